```python
import jax, jax.numpy as jnp
from jax import lax
import numpy as np

D_MODEL = 2048
BATCH = 8
SEQ = 2048
DEPTH = 1
DEC_BATCH = 128
DEC_SEQ = 1
PAST_LEN = 16384
PAGE_SIZE = 128

HEAD_DIM = 64
D_MIX = D_MODEL
D_RWKV = D_MIX // 2
D_ATTN = D_MIX - D_RWKV
N_RWKV_HEADS = D_RWKV // HEAD_DIM
N_Q_HEADS = D_ATTN // HEAD_DIM
N_KV_HEADS = 4
GROUP = N_Q_HEADS // N_KV_HEADS
WINDOW = 128
BLOCK = WINDOW
LORA_DECAY = 96
LORA_A = 96
LORA_GATE = 256
RWKV_PROJ = 3 * D_RWKV + LORA_DECAY + LORA_A + LORA_GATE
ATTN_PROJ = D_ATTN + 2 * N_KV_HEADS * HEAD_DIM
PROJ = RWKV_PROJ + ATTN_PROJ
D_FF = ((8 * D_MODEL + 3 * 256 - 1) // (3 * 256)) * 256
PLE_DIM = 256
EPS = 1e-6
GN_EPS = 64e-5
SCALE = HEAD_DIM ** -0.5
RWKV_SPLITS = [D_RWKV, 2 * D_RWKV, 3 * D_RWKV, 3 * D_RWKV + LORA_DECAY, 3 * D_RWKV + LORA_DECAY + LORA_A]
ATTN_SPLITS = [D_ATTN, D_ATTN + N_KV_HEADS * HEAD_DIM]

kernel_name = "hymba_rwkv7_swa_sink_alibi_decode_step"


def rms_norm(x, g):
    x32 = x.astype(jnp.float32)
    y = x32 * lax.rsqrt(jnp.mean(x32 * x32, axis=-1, keepdims=True) + EPS)
    return (y * g.astype(jnp.float32)).astype(x.dtype)


def alibi_slopes():
    return 2.0 ** (-8.0 * jnp.arange(1, N_Q_HEADS + 1, dtype=jnp.float32) / N_Q_HEADS)


def rwkv7_time_mix(zr, shift_prev, wkv0, mu, w0, w2, a0, a2, g2, k_k, k_a, r_k, ln_w, ln_b):
    b, t, _ = zr.shape
    z_prev = jnp.concatenate([shift_prev[:, None, :].astype(zr.dtype), zr[:, :-1]], axis=1)
    xs = zr + (z_prev - zr) * mu
    r, k, v, xw, xa, xg = jnp.split(xs, RWKV_SPLITS, axis=-1)
    w = -jax.nn.softplus(-(w0 + jnp.tanh(xw) @ w2)) - 0.5
    a = jax.nn.sigmoid(a0 + xa @ a2)
    g = jax.nn.sigmoid(xg) @ g2
    heads = lambda u: u.reshape(b, t, N_RWKV_HEADS, HEAD_DIM).astype(jnp.float32)
    kk = heads(k * k_k)
    kk = kk * lax.rsqrt(jnp.maximum(jnp.sum(kk * kk, axis=-1, keepdims=True), 1e-24))
    k_h = heads(k * (1 + (a - 1) * k_a))
    r_h, v_h, a_h = heads(r), heads(v), heads(a)
    decay = jnp.exp(-jnp.exp(heads(w)))

    def step(S, inp):
        r_t, d_t, k_t, v_t, kk_t, a_t = inp
        sa = jnp.einsum('bhvk,bhk->bhv', S, -kk_t)
        S = (S * d_t[:, :, None, :]
             + sa[..., None] * (kk_t * a_t)[:, :, None, :]
             + v_t[..., None] * k_t[:, :, None, :])
        return S, jnp.einsum('bhvk,bhk->bhv', S, r_t)

    seq_first = lambda u: jnp.swapaxes(u, 0, 1)
    s_fin, y = lax.scan(step, wkv0.astype(jnp.float32),
                        tuple(seq_first(u) for u in (r_h, decay, k_h, v_h, kk, a_h)))
    y = seq_first(y)
    mean = jnp.mean(y, axis=-1, keepdims=True)
    var = jnp.mean(jnp.square(y - mean), axis=-1, keepdims=True)
    y = ((y - mean) * lax.rsqrt(var + GN_EPS)).reshape(b, t, D_RWKV) * ln_w + ln_b
    bonus = (jnp.sum(r_h * k_h * r_k, axis=-1, keepdims=True) * v_h).reshape(b, t, D_RWKV)
    out = (y + bonus) * g
    return out.astype(zr.dtype), zr[:, -1], s_fin


def sink_softmax(scores, valid, sinks):
    scores = jnp.where(valid, scores, -jnp.inf)
    sink = sinks.astype(jnp.float32)[:, :, None, None]
    m = jnp.maximum(jnp.max(scores, axis=-1, keepdims=True), sink)
    p = jnp.exp(scores - m)
    return p / (jnp.sum(p, axis=-1, keepdims=True) + jnp.exp(sink - m))


def swa_prompt(q, k, v, sinks):
    b, t = q.shape[:2]
    nb = t // BLOCK
    qb = q.reshape(b, nb, BLOCK, N_KV_HEADS, GROUP, HEAD_DIM)

    def band(u):
        up = jnp.concatenate([jnp.zeros_like(u[:, :BLOCK]), u], axis=1)
        up = up.reshape(b, nb + 1, BLOCK, N_KV_HEADS, HEAD_DIM)
        return jnp.concatenate([up[:, :-1], up[:, 1:]], axis=2)

    kb, vb = band(k), band(v)
    s = jnp.einsum('bnqkgd,bnskd->bnkgqs', qb, kb, preferred_element_type=jnp.float32) * SCALE
    dist = jnp.arange(BLOCK)[:, None] + BLOCK - jnp.arange(2 * BLOCK)[None, :]
    key_pos = jnp.arange(nb)[:, None] * BLOCK - BLOCK + jnp.arange(2 * BLOCK)[None, :]
    valid = (dist >= 0) & (dist < WINDOW) & (key_pos >= 0)[:, None, None, None, :]
    s = s - alibi_slopes().reshape(N_KV_HEADS, GROUP, 1, 1) * dist.astype(jnp.float32)
    p = sink_softmax(s, valid, sinks.reshape(N_KV_HEADS, GROUP))
    o = jnp.einsum('bnkgqs,bnskd->bnqkgd', p.astype(vb.dtype), vb)
    return o.reshape(b, t, D_ATTN), k[:, -WINDOW:], v[:, -WINDOW:]


def swa_sample(q, k, v, k_buf, v_buf, sinks):
    b, t = q.shape[:2]
    kc = jnp.concatenate([k_buf.astype(k.dtype), k], axis=1)
    vc = jnp.concatenate([v_buf.astype(v.dtype), v], axis=1)
    qg = q.reshape(b, t, N_KV_HEADS, GROUP, HEAD_DIM)
    s = jnp.einsum('btkgd,bskd->bkgts', qg, kc, preferred_element_type=jnp.float32) * SCALE
    dist = jnp.arange(t)[:, None] + WINDOW - jnp.arange(WINDOW + t)[None, :]
    valid = (dist >= 0) & (dist < WINDOW)
    s = s - alibi_slopes().reshape(N_KV_HEADS, GROUP, 1, 1) * dist.astype(jnp.float32)
    p = sink_softmax(s, valid, sinks.reshape(N_KV_HEADS, GROUP))
    o = jnp.einsum('bkgts,bskd->btkgd', p.astype(vc.dtype), vc)
    return o.reshape(b, t, D_ATTN), kc[:, -WINDOW:], vc[:, -WINDOW:]


def decoder_layer(x, p_emb, shift_prev, wkv0, kv_buf, lp):
    b, t = x.shape[:2]
    h = rms_norm(x, lp['norm_mix'])
    z = h @ lp['w_in']
    zr, za = z[..., :RWKV_PROJ], z[..., RWKV_PROJ:]
    y_r, shift_new, wkv_new = rwkv7_time_mix(
        zr, shift_prev, wkv0, lp['mu_shift'], lp['rwkv_w0'], lp['rwkv_w2'], lp['rwkv_a0'],
        lp['rwkv_a2'], lp['rwkv_g2'], lp['rwkv_k_k'], lp['rwkv_k_a'], lp['rwkv_r_k'],
        lp['rwkv_ln_w'], lp['rwkv_ln_b'])
    q, k, v = jnp.split(za, ATTN_SPLITS, axis=-1)
    q = q.reshape(b, t, N_Q_HEADS, HEAD_DIM)
    k = k.reshape(b, t, N_KV_HEADS, HEAD_DIM)
    v = v.reshape(b, t, N_KV_HEADS, HEAD_DIM)
    if kv_buf is None:
        y_a, k_new, v_new = swa_prompt(q, k, v, lp['attn_sinks'])
    else:
        y_a, k_new, v_new = swa_sample(q, k, v, kv_buf[0], kv_buf[1], lp['attn_sinks'])
    x = x + jnp.concatenate([y_r, y_a], axis=-1) @ lp['w_out']
    hf = rms_norm(x, lp['norm_ffn'])
    x = x + (jax.nn.silu(hf @ lp['w_gate']) * (hf @ lp['w_up'])) @ lp['w_down']
    x = x + jax.nn.sigmoid(rms_norm(x, lp['norm_ple']) @ lp['ple_gate']) * (p_emb @ lp['ple_proj'])
    return x, wkv_new, shift_new, k_new, v_new


def setup_inputs(seed: int = 0) -> dict:
    key = jax.random.key(seed)
    ks = iter(jax.random.split(key, 40))
    nrm = lambda shape, scale: scale * jax.random.normal(next(ks), shape, jnp.float32)
    gain = lambda shape: 1.0 + 0.05 * jax.random.normal(next(ks), shape, jnp.float32)
    unif = lambda shape, lo, hi: jax.random.uniform(next(ks), shape, jnp.float32, lo, hi)
    return {
        'x_prompt': nrm((BATCH, SEQ, D_MODEL), 1.0),
        'x_sample': nrm((DEC_BATCH, DEC_SEQ, D_MODEL), 1.0),
        'state_wkv': nrm((DEPTH, DEC_BATCH, N_RWKV_HEADS, HEAD_DIM, HEAD_DIM), 0.3),
        'state_shift': nrm((DEPTH, DEC_BATCH, RWKV_PROJ), 1.0),
        'cache_k': nrm((DEPTH, DEC_BATCH, WINDOW, N_KV_HEADS, HEAD_DIM), 1.0),
        'cache_v': nrm((DEPTH, DEC_BATCH, WINDOW, N_KV_HEADS, HEAD_DIM), 1.0),
        'p_prompt': nrm((DEPTH, BATCH, SEQ, PLE_DIM), 1.0),
        'p_sample': nrm((DEPTH, DEC_BATCH, DEC_SEQ, PLE_DIM), 1.0),
        'norm_mix': gain((DEPTH, D_MODEL)),
        'w_in': nrm((DEPTH, D_MODEL, PROJ), D_MODEL ** -0.5),
        'mu_shift': unif((DEPTH, RWKV_PROJ), 0.0, 1.0),
        'rwkv_w0': unif((DEPTH, D_RWKV), -5.0, 0.0),
        'rwkv_w2': nrm((DEPTH, LORA_DECAY, D_RWKV), LORA_DECAY ** -0.5),
        'rwkv_a0': nrm((DEPTH, D_RWKV), 0.1),
        'rwkv_a2': nrm((DEPTH, LORA_A, D_RWKV), LORA_A ** -0.5),
        'rwkv_g2': nrm((DEPTH, LORA_GATE, D_RWKV), LORA_GATE ** -0.5),
        'rwkv_k_k': gain((DEPTH, D_RWKV)),
        'rwkv_k_a': gain((DEPTH, D_RWKV)),
        'rwkv_r_k': nrm((DEPTH, N_RWKV_HEADS, HEAD_DIM), 0.1),
        'rwkv_ln_w': gain((DEPTH, D_RWKV)),
        'rwkv_ln_b': nrm((DEPTH, D_RWKV), 0.02),
        'attn_sinks': nrm((DEPTH, N_Q_HEADS), 1.0),
        'w_out': nrm((DEPTH, D_MIX, D_MODEL), D_MIX ** -0.5),
        'norm_ffn': gain((DEPTH, D_MODEL)),
        'w_gate': nrm((DEPTH, D_MODEL, D_FF), D_MODEL ** -0.5),
        'w_up': nrm((DEPTH, D_MODEL, D_FF), D_MODEL ** -0.5),
        'w_down': nrm((DEPTH, D_FF, D_MODEL), D_FF ** -0.5),
        'norm_ple': gain((DEPTH, D_MODEL)),
        'ple_gate': nrm((DEPTH, D_MODEL, D_MODEL), D_MODEL ** -0.5),
        'ple_proj': nrm((DEPTH, PLE_DIM, D_MODEL), PLE_DIM ** -0.5),
        'final_norm': gain((D_MODEL,)),
    }


def reference(x_prompt, x_sample, state_wkv, state_shift, cache_k, cache_v, p_prompt, p_sample,
              norm_mix, w_in, mu_shift, rwkv_w0, rwkv_w2, rwkv_a0, rwkv_a2, rwkv_g2, rwkv_k_k,
              rwkv_k_a, rwkv_r_k, rwkv_ln_w, rwkv_ln_b, attn_sinks, w_out, norm_ffn, w_gate, w_up,
              w_down, norm_ple, ple_gate, ple_proj, final_norm):
    xp, xs = x_prompt, x_sample
    bp = x_prompt.shape[0]
    wkv_p, shift_p, k_p, v_p = [], [], [], []
    wkv_s, shift_s, k_s, v_s = [], [], [], []
    for i in range(DEPTH):
        lp = {
            'norm_mix': norm_mix[i], 'w_in': w_in[i], 'mu_shift': mu_shift[i],
            'rwkv_w0': rwkv_w0[i], 'rwkv_w2': rwkv_w2[i], 'rwkv_a0': rwkv_a0[i],
            'rwkv_a2': rwkv_a2[i], 'rwkv_g2': rwkv_g2[i], 'rwkv_k_k': rwkv_k_k[i],
            'rwkv_k_a': rwkv_k_a[i], 'rwkv_r_k': rwkv_r_k[i], 'rwkv_ln_w': rwkv_ln_w[i],
            'rwkv_ln_b': rwkv_ln_b[i], 'attn_sinks': attn_sinks[i], 'w_out': w_out[i],
            'norm_ffn': norm_ffn[i], 'w_gate': w_gate[i], 'w_up': w_up[i], 'w_down': w_down[i],
            'norm_ple': norm_ple[i], 'ple_gate': ple_gate[i], 'ple_proj': ple_proj[i],
        }
        shift0 = jnp.zeros((bp, RWKV_PROJ), xp.dtype)
        wkv_zero = jnp.zeros((bp, N_RWKV_HEADS, HEAD_DIM, HEAD_DIM), jnp.float32)
        xp, a1, a2, a3, a4 = decoder_layer(xp, p_prompt[i], shift0, wkv_zero, None, lp)
        wkv_p.append(a1); shift_p.append(a2); k_p.append(a3); v_p.append(a4)
        xs, b1, b2, b3, b4 = decoder_layer(xs, p_sample[i], state_shift[i], state_wkv[i],
                                           (cache_k[i], cache_v[i]), lp)
        wkv_s.append(b1); shift_s.append(b2); k_s.append(b3); v_s.append(b4)
    y_prompt = rms_norm(xp, final_norm)
    y_sample = rms_norm(xs, final_norm)
    return (y_prompt, y_sample,
            jnp.stack(wkv_p), jnp.stack(shift_p), jnp.stack(k_p), jnp.stack(v_p),
            jnp.stack(wkv_s), jnp.stack(shift_s), jnp.stack(k_s), jnp.stack(v_s))
```

```python
import functools
import math

import jax
import jax.numpy as jnp
from jax import lax
from jax.experimental import pallas as pl
from jax.experimental.pallas import tpu as pltpu

F32 = jnp.float32
BF16 = jnp.bfloat16

HEAD = 64
LANES = 128
N_KV = 4
WINDOW = 128
LORA_W = 96
LORA_A = 96
LORA_G = 256
LORA_PAD = 128
EPS = 1e-6
GN_EPS = 64e-5
SCALE = HEAD ** -0.5
CHUNK = 64
DECAY_SCALE = math.exp(-0.5)
VMEM_CAP = 60 * 1024 * 1024


def _cparams(sem, vmem_bytes):
    return pltpu.CompilerParams(dimension_semantics=sem,
                                vmem_limit_bytes=int(min(VMEM_CAP, vmem_bytes)))


def _dot(a, b):
    return jnp.dot(a.astype(BF16), b.astype(BF16), preferred_element_type=F32)


def _dot_nt(a, b):
    return lax.dot_general(a.astype(BF16), b.astype(BF16), (((1,), (1,)), ((), ())),
                           preferred_element_type=F32)


def _dot_tn(a, b):
    return lax.dot_general(a.astype(BF16), b.astype(BF16), (((0,), (0,)), ((), ())),
                           preferred_element_type=F32)


def _split2(x):
    hi = x.astype(BF16)
    lo = (x - hi.astype(F32)).astype(BF16)
    return hi, lo


def _split3(x):
    hi = x.astype(BF16)
    r1 = x - hi.astype(F32)
    mid = r1.astype(BF16)
    lo = (r1 - mid.astype(F32)).astype(BF16)
    return hi, mid, lo


def _head_ones():
    i = lax.broadcasted_iota(jnp.int32, (LANES, LANES), 0) >> 6
    j = lax.broadcasted_iota(jnp.int32, (LANES, LANES), 1) >> 6
    return jnp.where(i == j, 1.0, 0.0).astype(BF16)


def _segsum(x, ones_bd):
    hi, lo = _split2(x)
    return (jnp.dot(hi, ones_bd, preferred_element_type=F32)
            + jnp.dot(lo, ones_bd, preferred_element_type=F32))


def _rms(x, g):
    ms = jnp.mean(x * x, axis=-1, keepdims=True)
    return x * lax.rsqrt(ms + EPS) * g


def _sigmoid(x):
    return 1.0 / (1.0 + jnp.exp(-x))


def _norm_mm_kernel(x_ref, g_ref, w_ref, o_ref, h_ref):
    @pl.when(pl.program_id(1) == 0)
    def _():
        h_ref[...] = _rms(x_ref[...], g_ref[...]).astype(BF16)

    o_ref[...] = jnp.dot(h_ref[...], w_ref[...], preferred_element_type=F32)


def _norm_matmul(x, gain, w, tm, tn):
    m, k = x.shape
    n = w.shape[1]
    vmem = 2 * tm * k * 4 + 2 * k * tn * 2 + 2 * tm * tn * 4 + tm * k * 2 + (4 << 20)
    return pl.pallas_call(
        _norm_mm_kernel,
        grid=(m // tm, n // tn),
        in_specs=[pl.BlockSpec((tm, k), lambda i, j: (i, 0)),
                  pl.BlockSpec((1, k), lambda i, j: (0, 0)),
                  pl.BlockSpec((k, tn), lambda i, j: (0, j))],
        out_specs=pl.BlockSpec((tm, tn), lambda i, j: (i, j)),
        out_shape=jax.ShapeDtypeStruct((m, n), F32),
        scratch_shapes=[pltpu.VMEM((tm, k), BF16)],
        compiler_params=_cparams(("arbitrary", "arbitrary"), vmem),
        name="norm_in_proj",
    )(x, gain, w)


def _prep_core(zr, zpr, zl, zpl, mur_ref, mul_ref, w0_ref, w2_ref, a0_ref, a2_ref, g2_ref,
               kk_ref, ka_ref, out_refs):
    r_o, kh_o, v_o, kk_o, ba_o, lw_o, g_o = out_refs
    d = r_o.shape[-1]
    xr = zr + (zpr - zr) * mur_ref[...]
    xl = zl + (zpl - zl) * mul_ref[...]
    r = xr[:, 0:d]
    k = xr[:, d:2 * d]
    v = xr[:, 2 * d:3 * d]
    xw = xl[:, 0:LORA_PAD]
    xa = xl[:, LORA_PAD:2 * LORA_PAD]
    xg = xl[:, 2 * LORA_PAD:2 * LORA_PAD + LORA_G]
    u = w0_ref[...] + jnp.dot(jnp.tanh(xw).astype(BF16), w2_ref[...], preferred_element_type=F32)
    lw = -DECAY_SCALE * _sigmoid(u)
    a = _sigmoid(a0_ref[...] + jnp.dot(xa.astype(BF16), a2_ref[...], preferred_element_type=F32))
    g = jnp.dot(_sigmoid(xg).astype(BF16), g2_ref[...], preferred_element_type=F32)
    kk = k * kk_ref[...]
    ones_bd = _head_ones()
    kk2 = kk * kk
    ss = jnp.concatenate([_segsum(kk2[:, t * LANES:(t + 1) * LANES], ones_bd)
                          for t in range(d // LANES)], axis=1)
    kk = kk * lax.rsqrt(jnp.maximum(ss, 1e-24))
    kh = k * (1.0 + (a - 1.0) * ka_ref[...])
    r_o[...] = r
    kh_o[...] = kh
    v_o[...] = v
    kk_o[...] = kk
    ba_o[...] = kk * a
    lw_o[...] = lw
    g_o[...] = g


def _prep_prompt_kernel(zr_ref, zl_ref, mur, mul, w0, w2, a0, a2, g2, kkp, kap,
                        r_o, kh_o, v_o, kk_o, ba_o, lw_o, g_o, cr_ref, cl_ref):
    @pl.when(pl.program_id(1) == 0)
    def _():
        cr_ref[...] = jnp.zeros_like(cr_ref)
        cl_ref[...] = jnp.zeros_like(cl_ref)

    zr = zr_ref[...]
    zl = zl_ref[...]
    tt = zr.shape[0]
    first = lax.broadcasted_iota(jnp.int32, (tt, 1), 0) == 0
    zpr = jnp.where(first, cr_ref[0:1, :], pltpu.roll(zr, 1, 0))
    zpl = jnp.where(first, cl_ref[0:1, :], pltpu.roll(zl, 1, 0))
    cr_ref[0:1, :] = zr[tt - 1:tt, :]
    cl_ref[0:1, :] = zl[tt - 1:tt, :]
    _prep_core(zr, zpr, zl, zpl, mur, mul, w0, w2, a0, a2, g2, kkp, kap,
               (r_o, kh_o, v_o, kk_o, ba_o, lw_o, g_o))


def _prep_sample_kernel(zr_ref, zl_ref, pr_ref, pl_ref, mur, mul, w0, w2, a0, a2, g2, kkp, kap,
                        r_o, kh_o, v_o, kk_o, ba_o, lw_o, g_o):
    _prep_core(zr_ref[...], pr_ref[...], zl_ref[...], pl_ref[...], mur, mul, w0, w2, a0, a2, g2,
               kkp, kap, (r_o, kh_o, v_o, kk_o, ba_o, lw_o, g_o))


def _const_spec(shape):
    return pl.BlockSpec(shape, lambda *_: (0,) * len(shape))


def _prep_param_specs(d):
    return [_const_spec((1, 3 * d)), _const_spec((1, 512)), _const_spec((1, d)),
            _const_spec((LORA_PAD, d)), _const_spec((1, d)), _const_spec((LORA_PAD, d)),
            _const_spec((LORA_G, d)), _const_spec((1, d)), _const_spec((1, d))]


def _rwkv_prep_prompt(z, params, nb, t, d, tt, lora_blk):
    m = nb * t
    nt = t // tt
    row = lambda b, i: (b * nt + i, 0)
    out = jax.ShapeDtypeStruct((m, d), F32)
    vmem = 2 * tt * (3 * d + 512) * 4 + 14 * tt * d * 4 + 24 * tt * d * 4 + (8 << 20)
    return pl.pallas_call(
        _prep_prompt_kernel,
        grid=(nb, nt),
        in_specs=[pl.BlockSpec((tt, 3 * d), row),
                  pl.BlockSpec((tt, 512), lambda b, i: (b * nt + i, lora_blk))]
                 + _prep_param_specs(d),
        out_specs=[pl.BlockSpec((tt, d), row)] * 7,
        out_shape=[out] * 7,
        scratch_shapes=[pltpu.VMEM((8, 3 * d), F32), pltpu.VMEM((8, 512), F32)],
        compiler_params=_cparams(("arbitrary", "arbitrary"), vmem),
        name="rwkv_prep_prompt",
    )(z, z, *params)


def _rwkv_prep_sample(z, prev_r, prev_l, params, d, lora_blk):
    m = z.shape[0]
    out = jax.ShapeDtypeStruct((m, d), F32)
    vmem = 4 * m * (3 * d + 512) * 4 + 14 * m * d * 4 + 24 * m * d * 4 + (8 << 20)
    return pl.pallas_call(
        _prep_sample_kernel,
        grid=(1,),
        in_specs=[pl.BlockSpec((m, 3 * d), lambda i: (0, 0)),
                  pl.BlockSpec((m, 512), lambda i: (0, lora_blk)),
                  _const_spec((m, 3 * d)), _const_spec((m, 512))]
                 + _prep_param_specs(d),
        out_specs=[pl.BlockSpec((m, d), lambda i: (0, 0))] * 7,
        out_shape=[out] * 7,
        compiler_params=_cparams(("arbitrary",), vmem),
        name="rwkv_prep_sample",
    )(z, z, prev_r, prev_l, *params)


def _rwkv_post(y, r, kh, v, g, rk, lnw, lnb, ones_bd):
    inv = 1.0 / HEAD
    mean = _segsum(y, ones_bd) * inv
    dlt = y - mean
    var = _segsum(dlt * dlt, ones_bd) * inv
    yn = dlt * lax.rsqrt(var + GN_EPS) * lnw + lnb
    bonus = _segsum(r * kh * rk, ones_bd) * v
    return (yn + bonus) * g


def _pair_diag(x, lane_lo):
    return jnp.concatenate([jnp.where(lane_lo, x, 0.0), jnp.where(lane_lo, 0.0, x)], axis=0)


def _scan_kernel(r_ref, kh_ref, v_ref, kk_ref, ba_ref, lw_ref, g_ref, rk_ref, lnw_ref, lnb_ref,
                 o_ref, st_ref, s_scr):
    c = pl.program_id(1)
    n_pairs = s_scr.shape[0]
    C = CHUNK

    @pl.when(c == 0)
    def _():
        s_scr[...] = jnp.zeros_like(s_scr)

    row = lax.broadcasted_iota(jnp.int32, (C, LANES), 0)
    lane = lax.broadcasted_iota(jnp.int32, (C, LANES), 1)
    src = lane & (HEAD - 1)
    strict = src < row
    incl = src <= row
    eye_w = jnp.where(src == row, 1.0, 0.0)
    lane_lo = lane < HEAD
    sq_r = lax.broadcasted_iota(jnp.int32, (LANES, LANES), 0) >> 6
    sq_c = lax.broadcasted_iota(jnp.int32, (LANES, LANES), 1) >> 6
    same_head = sq_r == sq_c
    ones_bd = jnp.where(same_head, 1.0, 0.0).astype(BF16)
    tri = jnp.where(lax.broadcasted_iota(jnp.int32, (C, C), 1)
                    <= lax.broadcasted_iota(jnp.int32, (C, C), 0), 1.0, 0.0).astype(BF16)

    for p in range(n_pairs):
        sl = slice(p * LANES, (p + 1) * LANES)
        r = r_ref[:, sl]
        kh = kh_ref[:, sl]
        v = v_ref[:, sl]
        kk = kk_ref[:, sl]
        ba = ba_ref[:, sl]
        lw = lw_ref[:, sl]
        s_bd = s_scr[p]

        l_hi, l_mid, l_lo = _split3(lw)
        cs = (jnp.dot(tri, l_hi, preferred_element_type=F32)
              + jnp.dot(tri, l_mid, preferred_element_type=F32)
              + jnp.dot(tri, l_lo, preferred_element_type=F32))
        c_end = cs[C - 1:C, :]
        e_in = jnp.exp(cs)
        e_ex = jnp.exp(cs - lw)
        e_neg = jnp.exp(-cs)
        e_end = jnp.exp(c_end - cs)
        r_t = r * e_in
        k_x = kk * e_ex
        k_t = kh * e_neg
        b_t = ba * e_neg
        k_e = kh * e_end
        b_e = ba * e_end

        lhs = jnp.concatenate([k_x, r_t], axis=0)
        rhs = jnp.concatenate([_pair_diag(k_t, lane_lo), _pair_diag(b_t, lane_lo)], axis=0)
        gram = _dot_nt(lhs, rhs)
        a_kk = jnp.where(strict, gram[0:C, 0:LANES], 0.0)
        a_kb = jnp.where(strict, gram[0:C, LANES:2 * LANES], 0.0)
        a_rk = jnp.where(incl, gram[C:2 * C, 0:LANES], 0.0)
        a_rb = jnp.where(incl, gram[C:2 * C, LANES:2 * LANES], 0.0)

        pw = _dot(a_kb, _pair_diag(a_kb, lane_lo))
        t_inv = eye_w - a_kb
        n_lvl = int(math.log2(C)) - 1
        for lvl in range(n_lvl):
            stacked = jnp.concatenate([t_inv, pw], axis=0) if lvl < n_lvl - 1 else t_inv
            prod = _dot(stacked, _pair_diag(pw, lane_lo))
            t_inv = t_inv + prod[0:C]
            if lvl < n_lvl - 1:
                pw = prod[C:2 * C]

        v_bd = _pair_diag(v, lane_lo)
        w_mat = _dot_nt(k_x, s_bd) + _dot(a_kk, v_bd)
        u = _dot(t_inv, _pair_diag(w_mat, lane_lo))
        y = _dot_nt(r_t, s_bd) + _dot(a_rk, v_bd) - _dot(a_rb, _pair_diag(u, lane_lo))

        upd = _dot_tn(jnp.concatenate([v, u], axis=0), jnp.concatenate([k_e, -b_e], axis=0))
        s_new = s_bd * jnp.exp(c_end) + jnp.where(same_head, upd, 0.0)
        s_scr[p] = s_new

        o_ref[:, sl] = _rwkv_post(y, r, kh, v, g_ref[:, sl], rk_ref[:, sl], lnw_ref[:, sl],
                                  lnb_ref[:, sl], ones_bd)

    @pl.when(c == pl.num_programs(1) - 1)
    def _():
        for p in range(n_pairs):
            s = s_scr[p]
            st_ref[0, 2 * p] = s[0:HEAD, 0:HEAD]
            st_ref[0, 2 * p + 1] = s[HEAD:2 * HEAD, HEAD:2 * HEAD]


def _rwkv_scan(r, kh, v, kk, ba, lw, g, rk, lnw, lnb, nb, t, d):
    nc = t // CHUNK
    nh = d // HEAD
    blk = pl.BlockSpec((CHUNK, d), lambda b, c: (b * nc + c, 0))
    par = _const_spec((1, d))
    vmem = 16 * CHUNK * d * 4 + 3 * (d // LANES) * LANES * LANES * 4 + (16 << 20)
    return pl.pallas_call(
        _scan_kernel,
        grid=(nb, nc),
        in_specs=[blk] * 7 + [par] * 3,
        out_specs=[blk, pl.BlockSpec((1, nh, HEAD, HEAD), lambda b, c: (b, 0, 0, 0))],
        out_shape=[jax.ShapeDtypeStruct((nb * t, d), F32),
                   jax.ShapeDtypeStruct((nb, nh, HEAD, HEAD), F32)],
        scratch_shapes=[pltpu.VMEM((d // LANES, LANES, LANES), F32)],
        compiler_params=_cparams(("arbitrary", "arbitrary"), vmem),
        name="rwkv_scan",
    )(r, kh, v, kk, ba, lw, g, rk, lnw, lnb)


def _step_kernel(s_ref, r_ref, kh_ref, v_ref, kk_ref, ba_ref, lw_ref, g_ref, rk_ref, lnw_ref,
                 lnb_ref, so_ref, o_ref):
    bb, nh = r_ref.shape[0], r_ref.shape[1]
    eye = (lax.broadcasted_iota(jnp.int32, (HEAD, HEAD), 0)
           == lax.broadcasted_iota(jnp.int32, (HEAD, HEAD), 1))
    inv = 1.0 / HEAD

    def body(b, carry):
        r_b, kh_b, v_b = r_ref[b], kh_ref[b], v_ref[b]
        kk_b, ba_b, dec_b = kk_ref[b], ba_ref[b], jnp.exp(lw_ref[b])
        rows = []
        for h in range(nh):
            s = s_ref[b, h]
            sa = -jnp.sum(s * kk_b[h:h + 1], axis=-1, keepdims=True)
            v_col = jnp.sum(jnp.where(eye, v_b[h:h + 1], 0.0), axis=-1, keepdims=True)
            s_new = s * dec_b[h:h + 1] + sa * ba_b[h:h + 1] + v_col * kh_b[h:h + 1]
            so_ref[b, h] = s_new
            y_col = jnp.sum(s_new * r_b[h:h + 1], axis=-1, keepdims=True)
            rows.append(jnp.sum(jnp.where(eye, y_col, 0.0), axis=0, keepdims=True))
        y = jnp.concatenate(rows, axis=0)
        mean = jnp.sum(y, axis=-1, keepdims=True) * inv
        dlt = y - mean
        var = jnp.sum(dlt * dlt, axis=-1, keepdims=True) * inv
        yn = dlt * lax.rsqrt(var + GN_EPS) * lnw_ref[...] + lnb_ref[...]
        bonus = jnp.sum(r_b * kh_b * rk_ref[...], axis=-1, keepdims=True) * v_b
        o_ref[b] = (yn + bonus) * g_ref[b]
        return carry

    lax.fori_loop(0, bb, body, 0)


def _rwkv_step(state, r, kh, v, kk, ba, lw, g, rk, lnw, lnb, bb):
    nb, nh = state.shape[0], state.shape[1]
    vec = pl.BlockSpec((bb, nh, HEAD), lambda i: (i, 0, 0))
    st = pl.BlockSpec((bb, nh, HEAD, HEAD), lambda i: (i, 0, 0, 0))
    par = _const_spec((nh, HEAD))
    vmem = 4 * bb * nh * HEAD * LANES * 4 + (16 << 20)
    return pl.pallas_call(
        _step_kernel,
        grid=(nb // bb,),
        in_specs=[st] + [vec] * 7 + [par] * 3,
        out_specs=[st, vec],
        out_shape=[jax.ShapeDtypeStruct(state.shape, F32),
                   jax.ShapeDtypeStruct((nb, nh, HEAD), F32)],
        compiler_params=_cparams(("arbitrary",), vmem),
        name="rwkv_step",
    )(state, r, kh, v, kk, ba, lw, g, rk, lnw, lnb)


def _alibi_slope(h, n_heads):
    return 2.0 ** (-8.0 * (h + 1) / n_heads)


def _attn_prompt_kernel(sink_ref, q_ref, kp_ref, kc_ref, vp_ref, vc_ref, o_ref):
    n = pl.program_id(1)
    nq = q_ref.shape[1] // HEAD
    group = nq // N_KV
    q = q_ref[...]
    k2 = jnp.concatenate([kp_ref[...], kc_ref[...]], axis=0).astype(BF16)
    v2 = jnp.concatenate([vp_ref[...], vc_ref[...]], axis=0).astype(BF16)
    qi = lax.broadcasted_iota(jnp.int32, (WINDOW, 2 * WINDOW), 0)
    si = lax.broadcasted_iota(jnp.int32, (WINDOW, 2 * WINDOW), 1)
    dist = qi + WINDOW - si
    first_key = jnp.where(n > 0, 0, WINDOW)
    valid = (dist >= 0) & (dist < WINDOW) & (si >= first_key)
    distf = dist.astype(F32)
    outs = []
    for h in range(nq):
        j = h // group
        qh = q[:, h * HEAD:(h + 1) * HEAD]
        s = _dot_nt(qh, k2[:, j * HEAD:(j + 1) * HEAD]) * SCALE - _alibi_slope(h, nq) * distf
        s = jnp.where(valid, s, -jnp.inf)
        sink = sink_ref[h]
        mx = jnp.maximum(jnp.max(s, axis=-1, keepdims=True), sink)
        p = jnp.exp(s - mx)
        den = jnp.sum(p, axis=-1, keepdims=True) + jnp.exp(sink - mx)
        outs.append(_dot(p, v2[:, j * HEAD:(j + 1) * HEAD]) / den)
    o_ref[...] = jnp.concatenate(outs, axis=1)


def _attn_prompt(z, sinks, nb, t, dq, q_blk, k_blk, v_blk):
    nblk = t // WINDOW
    dkv = N_KV * HEAD
    cur = lambda col: (lambda b, n: (b * nblk + n, col))
    prev = lambda col: (lambda b, n: (b * nblk + jnp.maximum(n - 1, 0), col))
    vmem = 4 * WINDOW * (dq * 2 + 4 * dkv) * 4 + (24 << 20)
    return pl.pallas_call(
        _attn_prompt_kernel,
        grid=(nb, nblk),
        in_specs=[pl.BlockSpec(memory_space=pltpu.SMEM),
                  pl.BlockSpec((WINDOW, dq), cur(q_blk)),
                  pl.BlockSpec((WINDOW, dkv), prev(k_blk)),
                  pl.BlockSpec((WINDOW, dkv), cur(k_blk)),
                  pl.BlockSpec((WINDOW, dkv), prev(v_blk)),
                  pl.BlockSpec((WINDOW, dkv), cur(v_blk))],
        out_specs=pl.BlockSpec((WINDOW, dq), lambda b, n: (b * nblk + n, 0)),
        out_shape=jax.ShapeDtypeStruct((nb * t, dq), F32),
        compiler_params=_cparams(("arbitrary", "arbitrary"), vmem),
        name="attn_prompt",
    )(sinks, z, z, z, z, z)


def _attn_sample_kernel(q_ref, kn_ref, vn_ref, ck_ref, cv_ref, sink_ref, slope_ref,
                        o_ref, ko_ref, vo_ref):
    bb, nq = q_ref.shape[0], q_ref.shape[1]
    group = nq // N_KV
    dkv = N_KV * HEAD
    hrow = lax.broadcasted_iota(jnp.int32, (nq, dkv), 0) // group
    hlane = lax.broadcasted_iota(jnp.int32, (nq, dkv), 1) >> 6
    own = hrow == hlane
    si = lax.broadcasted_iota(jnp.int32, (nq, WINDOW), 1)
    distf = (WINDOW - si).astype(F32)
    valid = si >= 1
    sink = sink_ref[...]
    slope = slope_ref[...]

    def body(b, carry):
        q = q_ref[b]
        q_bd = jnp.where(own, jnp.concatenate([q] * N_KV, axis=1), 0.0)
        ck = ck_ref[b]
        cv = cv_ref[b]
        kn = kn_ref[b]
        vn = vn_ref[b]
        s_c = _dot_nt(q_bd, ck) * SCALE - slope * distf
        s_c = jnp.where(valid, s_c, -jnp.inf)
        s_n = jnp.sum(q_bd.astype(BF16).astype(F32) * kn.astype(BF16).astype(F32),
                      axis=-1, keepdims=True) * SCALE
        mx = jnp.maximum(jnp.maximum(jnp.max(s_c, axis=-1, keepdims=True), s_n), sink)
        p_c = jnp.exp(s_c - mx)
        p_n = jnp.exp(s_n - mx)
        den = jnp.sum(p_c, axis=-1, keepdims=True) + p_n + jnp.exp(sink - mx)
        o_all = (_dot(p_c, cv) + p_n.astype(BF16).astype(F32) * vn.astype(BF16).astype(F32)) / den
        o_ref[b] = jnp.concatenate(
            [o_all[j * group:(j + 1) * group, j * HEAD:(j + 1) * HEAD] for j in range(N_KV)], axis=0)
        ko_ref[b] = jnp.concatenate([ck[1:], kn], axis=0)
        vo_ref[b] = jnp.concatenate([cv[1:], vn], axis=0)
        return carry

    lax.fori_loop(0, bb, body, 0)


def _attn_sample(q, kn, vn, ck, cv, sinks, slopes, bb):
    nb, nq = q.shape[0], q.shape[1]
    dkv = N_KV * HEAD
    cache = pl.BlockSpec((bb, WINDOW, dkv), lambda i: (i, 0, 0))
    new = pl.BlockSpec((bb, 1, dkv), lambda i: (i, 0, 0))
    qs = pl.BlockSpec((bb, nq, HEAD), lambda i: (i, 0, 0))
    vmem = 8 * bb * WINDOW * dkv * 4 + (16 << 20)
    return pl.pallas_call(
        _attn_sample_kernel,
        grid=(nb // bb,),
        in_specs=[qs, new, new, cache, cache, _const_spec((nq, 1)), _const_spec((nq, 1))],
        out_specs=[qs, cache, cache],
        out_shape=[jax.ShapeDtypeStruct((nb, nq, HEAD), F32),
                   jax.ShapeDtypeStruct(ck.shape, F32),
                   jax.ShapeDtypeStruct(cv.shape, F32)],
        compiler_params=_cparams(("arbitrary",), vmem),
        name="attn_sample",
    )(q, kn, vn, ck, cv, sinks, slopes)


def _out_proj_kernel(a_ref, b_ref, x_ref, w_ref, o_ref):
    d = a_ref.shape[1]
    o_ref[...] = (x_ref[...]
                  + jnp.dot(a_ref[...].astype(BF16), w_ref[0:d, :], preferred_element_type=F32)
                  + jnp.dot(b_ref[...].astype(BF16), w_ref[d:, :], preferred_element_type=F32))


def _out_proj(a, b, x, w, tm):
    m, d = a.shape
    dm = x.shape[1]
    vmem = 4 * tm * d * 4 + 4 * tm * dm * 4 + 2 * 2 * d * dm * 2 + (8 << 20)
    return pl.pallas_call(
        _out_proj_kernel,
        grid=(m // tm,),
        in_specs=[pl.BlockSpec((tm, d), lambda i: (i, 0)),
                  pl.BlockSpec((tm, d), lambda i: (i, 0)),
                  pl.BlockSpec((tm, dm), lambda i: (i, 0)),
                  _const_spec((2 * d, dm))],
        out_specs=pl.BlockSpec((tm, dm), lambda i: (i, 0)),
        out_shape=jax.ShapeDtypeStruct((m, dm), F32),
        compiler_params=_cparams(("arbitrary",), vmem),
        name="out_proj",
    )(a, b, x, w)


def _ffn_kernel(x_ref, g_ref, wg_ref, wu_ref, wd_ref, o_ref, h_ref, acc_ref):
    j = pl.program_id(1)

    @pl.when(j == 0)
    def _():
        h_ref[...] = _rms(x_ref[...], g_ref[...]).astype(BF16)
        acc_ref[...] = jnp.zeros_like(acc_ref)

    h = h_ref[...]
    gate = jnp.dot(h, wg_ref[...], preferred_element_type=F32)
    up = jnp.dot(h, wu_ref[...], preferred_element_type=F32)
    act = gate * _sigmoid(gate) * up
    acc_ref[...] += jnp.dot(act.astype(BF16), wd_ref[...], preferred_element_type=F32)

    @pl.when(j == pl.num_programs(1) - 1)
    def _():
        o_ref[...] = x_ref[...] + acc_ref[...]


def _ffn(x, gain, wg, wu, wd, tm, tf):
    m, dm = x.shape
    f = wg.shape[1]
    vmem = 4 * tm * dm * 4 + tm * dm * 2 + tm * dm * 4 + 6 * dm * tf * 2 + 4 * tm * tf * 4 + (6 << 20)
    return pl.pallas_call(
        _ffn_kernel,
        grid=(m // tm, f // tf),
        in_specs=[pl.BlockSpec((tm, dm), lambda i, j: (i, 0)),
                  _const_spec((1, dm)),
                  pl.BlockSpec((dm, tf), lambda i, j: (0, j)),
                  pl.BlockSpec((dm, tf), lambda i, j: (0, j)),
                  pl.BlockSpec((tf, dm), lambda i, j: (j, 0))],
        out_specs=pl.BlockSpec((tm, dm), lambda i, j: (i, 0)),
        out_shape=jax.ShapeDtypeStruct((m, dm), F32),
        scratch_shapes=[pltpu.VMEM((tm, dm), BF16), pltpu.VMEM((tm, dm), F32)],
        compiler_params=_cparams(("arbitrary", "arbitrary"), vmem),
        name="ffn",
    )(x, gain, wg, wu, wd)


def _ple_kernel(x_ref, p_ref, gn_ref, wgate_ref, wproj_ref, fn_ref, o_ref):
    x = x_ref[...]
    gate = _sigmoid(jnp.dot(_rms(x, gn_ref[...]).astype(BF16), wgate_ref[...],
                            preferred_element_type=F32))
    emb = jnp.dot(p_ref[...].astype(BF16), wproj_ref[...], preferred_element_type=F32)
    o_ref[...] = _rms(x + gate * emb, fn_ref[...])


def _ple_final(x, p, gn, wgate, wproj, fn, tm):
    m, dm = x.shape
    dp = p.shape[1]
    vmem = 4 * tm * dm * 4 + 2 * tm * dp * 4 + 2 * dm * dm * 2 + 2 * dp * dm * 2 + 4 * tm * dm * 4 + (6 << 20)
    return pl.pallas_call(
        _ple_kernel,
        grid=(m // tm,),
        in_specs=[pl.BlockSpec((tm, dm), lambda i: (i, 0)),
                  pl.BlockSpec((tm, dp), lambda i: (i, 0)),
                  _const_spec((1, dm)), _const_spec((dm, dm)), _const_spec((dp, dm)),
                  _const_spec((1, dm))],
        out_specs=pl.BlockSpec((tm, dm), lambda i: (i, 0)),
        out_shape=jax.ShapeDtypeStruct((m, dm), F32),
        compiler_params=_cparams(("arbitrary",), vmem),
        name="ple_final",
    )(x, p, gn, wgate, wproj, fn)


def _pad_cols(a, n):
    return jnp.pad(a, ((0, 0), (0, n - a.shape[1])))


def _layer(x, p_emb, lw_, final_norm, *, nb, t, state, shift_prev, caches):
    m, dm = x.shape
    d = lw_["d"]
    dq = lw_["dq"]
    dkv = N_KV * HEAD
    sample = caches is not None
    o_q, o_k, o_v, o_l = 3 * d, 3 * d + dq, 3 * d + dq + dkv, 3 * d + dq + 2 * dkv
    tm = min(m, 1024)
    z = _norm_matmul(x, lw_["norm_mix"], lw_["w_in"], tm, 1024)
    lora_blk = o_l // 512
    nh = d // HEAD
    if not sample:
        feats = _rwkv_prep_prompt(z, lw_["prep"], nb, t, d, 256, lora_blk)
        out_r, wkv = _rwkv_scan(*feats, lw_["r_k"], lw_["ln_w"], lw_["ln_b"], nb, t, d)
        y_a = _attn_prompt(z, lw_["sinks"], nb, t, dq, o_q // dq, o_k // dkv, o_v // dkv)
        z3 = z.reshape(nb, t, -1)
        shift_rows = z3[:, -1]
        k_new = z3[:, -WINDOW:, o_k:o_k + dkv].reshape(nb, WINDOW, N_KV, HEAD)
        v_new = z3[:, -WINDOW:, o_v:o_v + dkv].reshape(nb, WINDOW, N_KV, HEAD)
    else:
        prev_r = shift_prev[:, :3 * d]
        prev_l = jnp.concatenate(
            [_pad_cols(shift_prev[:, 3 * d:3 * d + LORA_W], LORA_PAD),
             _pad_cols(shift_prev[:, 3 * d + LORA_W:3 * d + LORA_W + LORA_A], LORA_PAD),
             shift_prev[:, 3 * d + LORA_W + LORA_A:]], axis=1)
        feats = _rwkv_prep_sample(z, prev_r, prev_l, lw_["prep"], d, lora_blk)
        feats3 = [f.reshape(m, nh, HEAD) for f in feats]
        wkv, out_r3 = _rwkv_step(state, *feats3, lw_["r_k"].reshape(nh, HEAD),
                                 lw_["ln_w"].reshape(nh, HEAD), lw_["ln_b"].reshape(nh, HEAD), 8)
        out_r = out_r3.reshape(m, d)
        ck, cv = caches
        q3 = z[:, o_q:o_q + dq].reshape(m, dq // HEAD, HEAD)
        kn = z[:, o_k:o_k + dkv].reshape(m, 1, dkv)
        vn = z[:, o_v:o_v + dkv].reshape(m, 1, dkv)
        y3, k_new, v_new = _attn_sample(q3, kn, vn, ck.reshape(m, WINDOW, dkv),
                                        cv.reshape(m, WINDOW, dkv), lw_["sinks"].reshape(-1, 1),
                                        lw_["slopes"], 8)
        y_a = y3.reshape(m, dq)
        k_new = k_new.reshape(m, WINDOW, N_KV, HEAD)
        v_new = v_new.reshape(m, WINDOW, N_KV, HEAD)
        shift_rows = z
    shift_new = jnp.concatenate(
        [shift_rows[:, :3 * d], shift_rows[:, o_l:o_l + LORA_W],
         shift_rows[:, o_l + LORA_PAD:o_l + LORA_PAD + LORA_A],
         shift_rows[:, o_l + 2 * LORA_PAD:o_l + 2 * LORA_PAD + LORA_G]], axis=1)
    tm2 = min(m, 512)
    x1 = _out_proj(out_r, y_a, x, lw_["w_out"], tm2)
    x2 = _ffn(x1, lw_["norm_ffn"], lw_["w_gate"], lw_["w_up"], lw_["w_down"], tm2, 512)
    y = _ple_final(x2, p_emb, lw_["norm_ple"], lw_["ple_gate"], lw_["ple_proj"], final_norm, tm2)
    return y, wkv, shift_new, k_new, v_new


def kernel(x_prompt, x_sample, state_wkv, state_shift, cache_k, cache_v, p_prompt, p_sample, norm_mix, w_in, mu_shift, rwkv_w0, rwkv_w2, rwkv_a0, rwkv_a2, rwkv_g2, rwkv_k_k, rwkv_k_a, rwkv_r_k, rwkv_ln_w, rwkv_ln_b, attn_sinks, w_out, norm_ffn, w_gate, w_up, w_down, norm_ple, ple_gate, ple_proj, final_norm):
    depth = w_in.shape[0]
    assert depth == 1, "single-layer step"
    bp, t, dm = x_prompt.shape
    bs = x_sample.shape[0]
    assert x_sample.shape[1] == 1
    d = rwkv_w0.shape[1]
    dq = attn_sinks.shape[1] * HEAD
    dkv = N_KV * HEAD
    i = 0
    row = lambda a: a.reshape(1, -1)
    wi = w_in[i]
    rp = 3 * d + LORA_W + LORA_A + LORA_G
    w_in_p = jnp.concatenate(
        [wi[:, :3 * d], wi[:, rp:],
         _pad_cols(wi[:, 3 * d:3 * d + LORA_W], LORA_PAD),
         _pad_cols(wi[:, 3 * d + LORA_W:3 * d + LORA_W + LORA_A], LORA_PAD),
         wi[:, 3 * d + LORA_W + LORA_A:rp]], axis=1).astype(BF16)
    mu = row(mu_shift[i])
    mu_r = mu[:, :3 * d]
    mu_l = jnp.concatenate(
        [_pad_cols(mu[:, 3 * d:3 * d + LORA_W], LORA_PAD),
         _pad_cols(mu[:, 3 * d + LORA_W:3 * d + LORA_W + LORA_A], LORA_PAD),
         mu[:, 3 * d + LORA_W + LORA_A:]], axis=1)
    pad_rows = lambda a: jnp.pad(a, ((0, LORA_PAD - a.shape[0]), (0, 0))).astype(BF16)
    nq = dq // HEAD
    lw_ = {
        "d": d, "dq": dq,
        "norm_mix": row(norm_mix[i]), "w_in": w_in_p,
        "prep": (mu_r, mu_l, row(rwkv_w0[i]), pad_rows(rwkv_w2[i]), row(rwkv_a0[i]),
                 pad_rows(rwkv_a2[i]), rwkv_g2[i].astype(BF16), row(rwkv_k_k[i]), row(rwkv_k_a[i])),
        "r_k": row(rwkv_r_k[i]), "ln_w": row(rwkv_ln_w[i]), "ln_b": row(rwkv_ln_b[i]),
        "sinks": attn_sinks[i],
        "slopes": (2.0 ** (-8.0 * jnp.arange(1, nq + 1, dtype=F32) / nq)).reshape(nq, 1),
        "w_out": w_out[i].astype(BF16), "norm_ffn": row(norm_ffn[i]),
        "w_gate": w_gate[i].astype(BF16), "w_up": w_up[i].astype(BF16),
        "w_down": w_down[i].astype(BF16), "norm_ple": row(norm_ple[i]),
        "ple_gate": ple_gate[i].astype(BF16), "ple_proj": ple_proj[i].astype(BF16),
    }
    fn = row(final_norm)
    yp, wkv_p, shift_p, k_p, v_p = _layer(
        x_prompt.reshape(bp * t, dm), p_prompt[i].reshape(bp * t, -1), lw_, fn,
        nb=bp, t=t, state=None, shift_prev=None, caches=None)
    ys, wkv_s, shift_s, k_s, v_s = _layer(
        x_sample.reshape(bs, dm), p_sample[i].reshape(bs, -1), lw_, fn,
        nb=bs, t=1, state=state_wkv[i], shift_prev=state_shift[i],
        caches=(cache_k[i], cache_v[i]))
    return (yp.reshape(bp, t, dm), ys.reshape(bs, 1, dm),
            wkv_p[None], shift_p[None], k_p[None], v_p[None],
            wkv_s[None], shift_s[None], k_s[None], v_s[None])
```

```python
import functools
import math

import jax
import jax.numpy as jnp
from jax import lax
from jax.experimental import pallas as pl
from jax.experimental.pallas import tpu as pltpu

F32 = jnp.float32
BF16 = jnp.bfloat16

HEAD = 64
LANES = 128
N_KV = 4
WINDOW = 128
LORA_W = 96
LORA_A = 96
LORA_G = 256
LORA_PAD = 128
EPS = 1e-6
GN_EPS = 64e-5
SCALE = HEAD ** -0.5
CHUNK = 64
DECAY_SCALE = math.exp(-0.5)
VMEM_CAP = 60 * 1024 * 1024


def _cparams(sem, vmem_bytes):
    return pltpu.CompilerParams(dimension_semantics=sem,
                                vmem_limit_bytes=int(min(VMEM_CAP, vmem_bytes)))


def _dot(a, b):
    return jnp.dot(a.astype(BF16), b.astype(BF16), preferred_element_type=F32)


def _dot_nt(a, b):
    return lax.dot_general(a.astype(BF16), b.astype(BF16), (((1,), (1,)), ((), ())),
                           preferred_element_type=F32)


def _dot_tn(a, b):
    return lax.dot_general(a.astype(BF16), b.astype(BF16), (((0,), (0,)), ((), ())),
                           preferred_element_type=F32)


def _split2(x):
    hi = x.astype(BF16)
    lo = (x - hi.astype(F32)).astype(BF16)
    return hi, lo


def _split3(x):
    hi = x.astype(BF16)
    r1 = x - hi.astype(F32)
    mid = r1.astype(BF16)
    lo = (r1 - mid.astype(F32)).astype(BF16)
    return hi, mid, lo


def _head_ones():
    i = lax.broadcasted_iota(jnp.int32, (LANES, LANES), 0) >> 6
    j = lax.broadcasted_iota(jnp.int32, (LANES, LANES), 1) >> 6
    return jnp.where(i == j, 1.0, 0.0).astype(BF16)


def _segsum(x, ones_bd):
    hi, lo = _split2(x)
    return (jnp.dot(hi, ones_bd, preferred_element_type=F32)
            + jnp.dot(lo, ones_bd, preferred_element_type=F32))


def _rms(x, g):
    ms = jnp.mean(x * x, axis=-1, keepdims=True)
    return x * lax.rsqrt(ms + EPS) * g


def _sigmoid(x):
    return 1.0 / (1.0 + jnp.exp(-x))


def _norm_mm_kernel(x_ref, g_ref, w_ref, o_ref, h_ref):
    @pl.when(pl.program_id(1) == 0)
    def _():
        h_ref[...] = _rms(x_ref[...], g_ref[...]).astype(BF16)

    o_ref[...] = jnp.dot(h_ref[...], w_ref[...], preferred_element_type=F32)


def _norm_matmul(x, gain, w, tm, tn):
    m, k = x.shape
    n = w.shape[1]
    vmem = 2 * tm * k * 4 + 2 * k * tn * 2 + 2 * tm * tn * 4 + tm * k * 2 + (4 << 20)
    return pl.pallas_call(
        _norm_mm_kernel,
        grid=(m // tm, n // tn),
        in_specs=[pl.BlockSpec((tm, k), lambda i, j: (i, 0)),
                  pl.BlockSpec((1, k), lambda i, j: (0, 0)),
                  pl.BlockSpec((k, tn), lambda i, j: (0, j))],
        out_specs=pl.BlockSpec((tm, tn), lambda i, j: (i, j)),
        out_shape=jax.ShapeDtypeStruct((m, n), F32),
        scratch_shapes=[pltpu.VMEM((tm, k), BF16)],
        compiler_params=_cparams(("arbitrary", "arbitrary"), vmem),
        name="norm_in_proj",
    )(x, gain, w)


def _prep_core(zr, zpr, zl, zpl, mur_ref, mul_ref, w0_ref, w2_ref, a0_ref, a2_ref, g2_ref,
               kk_ref, ka_ref, out_refs):
    r_o, kh_o, v_o, kk_o, ba_o, lw_o, g_o = out_refs
    d = r_o.shape[-1]
    xr = zr + (zpr - zr) * mur_ref[...]
    xl = zl + (zpl - zl) * mul_ref[...]
    r = xr[:, 0:d]
    k = xr[:, d:2 * d]
    v = xr[:, 2 * d:3 * d]
    xw = xl[:, 0:LORA_PAD]
    xa = xl[:, LORA_PAD:2 * LORA_PAD]
    xg = xl[:, 2 * LORA_PAD:2 * LORA_PAD + LORA_G]
    u = w0_ref[...] + jnp.dot(jnp.tanh(xw).astype(BF16), w2_ref[...], preferred_element_type=F32)
    lw = -DECAY_SCALE * _sigmoid(u)
    a = _sigmoid(a0_ref[...] + jnp.dot(xa.astype(BF16), a2_ref[...], preferred_element_type=F32))
    g = jnp.dot(_sigmoid(xg).astype(BF16), g2_ref[...], preferred_element_type=F32)
    kk = k * kk_ref[...]
    ones_bd = _head_ones()
    kk2 = kk * kk
    ss = jnp.concatenate([_segsum(kk2[:, t * LANES:(t + 1) * LANES], ones_bd)
                          for t in range(d // LANES)], axis=1)
    kk = kk * lax.rsqrt(jnp.maximum(ss, 1e-24))
    kh = k * (1.0 + (a - 1.0) * ka_ref[...])
    r_o[...] = r
    kh_o[...] = kh
    v_o[...] = v
    kk_o[...] = kk
    ba_o[...] = kk * a
    lw_o[...] = lw
    g_o[...] = g


def _prep_prompt_kernel(zr_ref, zl_ref, mur, mul, w0, w2, a0, a2, g2, kkp, kap,
                        r_o, kh_o, v_o, kk_o, ba_o, lw_o, g_o, cr_ref, cl_ref):
    @pl.when(pl.program_id(1) == 0)
    def _():
        cr_ref[...] = jnp.zeros_like(cr_ref)
        cl_ref[...] = jnp.zeros_like(cl_ref)

    zr = zr_ref[...]
    zl = zl_ref[...]
    tt = zr.shape[0]
    first = lax.broadcasted_iota(jnp.int32, (tt, 1), 0) == 0
    zpr = jnp.where(first, cr_ref[0:1, :], pltpu.roll(zr, 1, 0))
    zpl = jnp.where(first, cl_ref[0:1, :], pltpu.roll(zl, 1, 0))
    cr_ref[0:1, :] = zr[tt - 1:tt, :]
    cl_ref[0:1, :] = zl[tt - 1:tt, :]
    _prep_core(zr, zpr, zl, zpl, mur, mul, w0, w2, a0, a2, g2, kkp, kap,
               (r_o, kh_o, v_o, kk_o, ba_o, lw_o, g_o))


def _prep_sample_kernel(zr_ref, zl_ref, pr_ref, pl_ref, mur, mul, w0, w2, a0, a2, g2, kkp, kap,
                        r_o, kh_o, v_o, kk_o, ba_o, lw_o, g_o):
    _prep_core(zr_ref[...], pr_ref[...], zl_ref[...], pl_ref[...], mur, mul, w0, w2, a0, a2, g2,
               kkp, kap, (r_o, kh_o, v_o, kk_o, ba_o, lw_o, g_o))


def _const_spec(shape):
    return pl.BlockSpec(shape, lambda *_: (0,) * len(shape))


def _prep_param_specs(d):
    return [_const_spec((1, 3 * d)), _const_spec((1, 512)), _const_spec((1, d)),
            _const_spec((LORA_PAD, d)), _const_spec((1, d)), _const_spec((LORA_PAD, d)),
            _const_spec((LORA_G, d)), _const_spec((1, d)), _const_spec((1, d))]


def _rwkv_prep_prompt(z, params, nb, t, d, tt, lora_blk):
    m = nb * t
    nt = t // tt
    row = lambda b, i: (b * nt + i, 0)
    out = jax.ShapeDtypeStruct((m, d), F32)
    vmem = 2 * tt * (3 * d + 512) * 4 + 14 * tt * d * 4 + 24 * tt * d * 4 + (8 << 20)
    return pl.pallas_call(
        _prep_prompt_kernel,
        grid=(nb, nt),
        in_specs=[pl.BlockSpec((tt, 3 * d), row),
                  pl.BlockSpec((tt, 512), lambda b, i: (b * nt + i, lora_blk))]
                 + _prep_param_specs(d),
        out_specs=[pl.BlockSpec((tt, d), row)] * 7,
        out_shape=[out] * 7,
        scratch_shapes=[pltpu.VMEM((8, 3 * d), F32), pltpu.VMEM((8, 512), F32)],
        compiler_params=_cparams(("arbitrary", "arbitrary"), vmem),
        name="rwkv_prep_prompt",
    )(z, z, *params)


def _rwkv_prep_sample(z, prev_r, prev_l, params, d, lora_blk):
    m = z.shape[0]
    out = jax.ShapeDtypeStruct((m, d), F32)
    vmem = 4 * m * (3 * d + 512) * 4 + 14 * m * d * 4 + 24 * m * d * 4 + (8 << 20)
    return pl.pallas_call(
        _prep_sample_kernel,
        grid=(1,),
        in_specs=[pl.BlockSpec((m, 3 * d), lambda i: (0, 0)),
                  pl.BlockSpec((m, 512), lambda i: (0, lora_blk)),
                  _const_spec((m, 3 * d)), _const_spec((m, 512))]
                 + _prep_param_specs(d),
        out_specs=[pl.BlockSpec((m, d), lambda i: (0, 0))] * 7,
        out_shape=[out] * 7,
        compiler_params=_cparams(("arbitrary",), vmem),
        name="rwkv_prep_sample",
    )(z, z, prev_r, prev_l, *params)


def _rwkv_post(y, r, kh, v, g, rk, lnw, lnb, ones_bd, n_tiles):
    rows = y.shape[0] // n_tiles
    par = lambda a: jnp.concatenate(
        [jnp.broadcast_to(a[:, p * LANES:(p + 1) * LANES], (rows, LANES)) for p in range(n_tiles)], axis=0)
    inv = 1.0 / HEAD
    mean = _segsum(y, ones_bd) * inv
    dlt = y - mean
    var = _segsum(dlt * dlt, ones_bd) * inv
    yn = dlt * lax.rsqrt(var + GN_EPS) * par(lnw) + par(lnb)
    bonus = _segsum(r * kh * par(rk), ones_bd) * v
    return (yn + bonus) * g


def _pair_diag(x, lane_lo):
    return jnp.concatenate([jnp.where(lane_lo, x, 0.0), jnp.where(lane_lo, 0.0, x)], axis=0)


def _scan_kernel(r_ref, kh_ref, v_ref, kk_ref, ba_ref, lw_ref, g_ref, rk_ref, lnw_ref, lnb_ref,
                 o_ref, st_ref, s_scr):
    c = pl.program_id(1)
    n_pairs = s_scr.shape[0]
    C = CHUNK

    @pl.when(c == 0)
    def _():
        s_scr[...] = jnp.zeros_like(s_scr)

    row = lax.broadcasted_iota(jnp.int32, (C, LANES), 0)
    lane = lax.broadcasted_iota(jnp.int32, (C, LANES), 1)
    src = lane & (HEAD - 1)
    strict = src < row
    incl = src <= row
    eye_w = jnp.where(src == row, 1.0, 0.0)
    lane_lo = lane < HEAD
    sq_r = lax.broadcasted_iota(jnp.int32, (LANES, LANES), 0) >> 6
    sq_c = lax.broadcasted_iota(jnp.int32, (LANES, LANES), 1) >> 6
    same_head = sq_r == sq_c
    ones_bd = jnp.where(same_head, 1.0, 0.0).astype(BF16)
    tri = jnp.where(lax.broadcasted_iota(jnp.int32, (C, C), 1)
                    <= lax.broadcasted_iota(jnp.int32, (C, C), 0), 1.0, 0.0).astype(BF16)

    pairs = range(n_pairs)
    tile = lambda x, p: x[:, p * LANES:(p + 1) * LANES]
    rows_of = lambda x: jnp.concatenate([tile(x, p) for p in pairs], axis=0)
    lanes_of = lambda x: jnp.concatenate([x[p * C:(p + 1) * C] for p in pairs], axis=1)
    pd = lambda x: _pair_diag(x, lane_lo)

    r, kh, v = r_ref[...], kh_ref[...], v_ref[...]
    kk, ba, lw = kk_ref[...], ba_ref[...], lw_ref[...]
    l_hi, l_mid, l_lo = _split3(lw)
    cs = (jnp.dot(tri, l_hi, preferred_element_type=F32)
          + jnp.dot(tri, l_mid, preferred_element_type=F32)
          + jnp.dot(tri, l_lo, preferred_element_type=F32))
    c_end = cs[C - 1:C, :]
    e_neg = jnp.exp(-cs)
    e_end = jnp.exp(c_end - cs)
    r_t = r * jnp.exp(cs)
    k_x = kk * jnp.exp(cs - lw)
    k_t = kh * e_neg
    b_t = ba * e_neg
    k_e = kh * e_end
    b_e = ba * e_end
    g_end = jnp.exp(c_end)

    gram = [_dot_nt(jnp.concatenate([tile(k_x, p), tile(r_t, p)], axis=0),
                    jnp.concatenate([pd(tile(k_t, p)), pd(tile(b_t, p))], axis=0)) for p in pairs]
    a_kk = [jnp.where(strict, gm[0:C, 0:LANES], 0.0) for gm in gram]
    a_kb = [jnp.where(strict, gm[0:C, LANES:2 * LANES], 0.0) for gm in gram]
    a_rk = [jnp.where(incl, gm[C:2 * C, 0:LANES], 0.0) for gm in gram]
    a_rb = [jnp.where(incl, gm[C:2 * C, LANES:2 * LANES], 0.0) for gm in gram]

    pw = [_dot(a, pd(a)) for a in a_kb]
    t_inv = [eye_w - a for a in a_kb]
    n_lvl = int(math.log2(C)) - 1
    for lvl in range(n_lvl):
        last = lvl == n_lvl - 1
        prod = [_dot(t_inv[p] if last else jnp.concatenate([t_inv[p], pw[p]], axis=0), pd(pw[p]))
                for p in pairs]
        t_inv = [t_inv[p] + prod[p][0:C] for p in pairs]
        if not last:
            pw = [prod[p][C:2 * C] for p in pairs]

    s_bd = [s_scr[p] for p in pairs]
    v_bd = [pd(tile(v, p)) for p in pairs]
    w_mat = [_dot_nt(tile(k_x, p), s_bd[p]) + _dot(a_kk[p], v_bd[p]) for p in pairs]
    u = [_dot(t_inv[p], pd(w_mat[p])) for p in pairs]
    y = [_dot_nt(tile(r_t, p), s_bd[p]) + _dot(a_rk[p], v_bd[p]) - _dot(a_rb[p], pd(u[p]))
         for p in pairs]
    for p in pairs:
        upd = _dot_tn(jnp.concatenate([tile(v, p), u[p]], axis=0),
                      jnp.concatenate([tile(k_e, p), -tile(b_e, p)], axis=0))
        s_scr[p] = s_bd[p] * tile(g_end, p) + jnp.where(same_head, upd, 0.0)

    y_rows = jnp.concatenate(y, axis=0)
    out = _rwkv_post(y_rows, rows_of(r), rows_of(kh), rows_of(v), rows_of(g_ref[...]),
                     rk_ref[...], lnw_ref[...], lnb_ref[...], ones_bd, n_pairs)
    o_ref[...] = lanes_of(out)

    @pl.when(c == pl.num_programs(1) - 1)
    def _():
        for p in range(n_pairs):
            s = s_scr[p]
            st_ref[0, 2 * p] = s[0:HEAD, 0:HEAD]
            st_ref[0, 2 * p + 1] = s[HEAD:2 * HEAD, HEAD:2 * HEAD]


def _rwkv_scan(r, kh, v, kk, ba, lw, g, rk, lnw, lnb, nb, t, d):
    nc = t // CHUNK
    nh = d // HEAD
    blk = pl.BlockSpec((CHUNK, d), lambda b, c: (b * nc + c, 0))
    par = _const_spec((1, d))
    vmem = 16 * CHUNK * d * 4 + 3 * (d // LANES) * LANES * LANES * 4 + (16 << 20)
    return pl.pallas_call(
        _scan_kernel,
        grid=(nb, nc),
        in_specs=[blk] * 7 + [par] * 3,
        out_specs=[blk, pl.BlockSpec((1, nh, HEAD, HEAD), lambda b, c: (b, 0, 0, 0))],
        out_shape=[jax.ShapeDtypeStruct((nb * t, d), F32),
                   jax.ShapeDtypeStruct((nb, nh, HEAD, HEAD), F32)],
        scratch_shapes=[pltpu.VMEM((d // LANES, LANES, LANES), F32)],
        compiler_params=_cparams(("arbitrary", "arbitrary"), vmem),
        name="rwkv_scan",
    )(r, kh, v, kk, ba, lw, g, rk, lnw, lnb)


def _step_kernel(s_ref, r_ref, kh_ref, v_ref, kk_ref, ba_ref, lw_ref, g_ref, rk_ref, lnw_ref,
                 lnb_ref, so_ref, o_ref):
    bb, nh = r_ref.shape[0], r_ref.shape[1]
    eye = (lax.broadcasted_iota(jnp.int32, (HEAD, HEAD), 0)
           == lax.broadcasted_iota(jnp.int32, (HEAD, HEAD), 1))
    inv = 1.0 / HEAD

    def body(b, carry):
        r_b, kh_b, v_b = r_ref[b], kh_ref[b], v_ref[b]
        kk_b, ba_b, dec_b = kk_ref[b], ba_ref[b], jnp.exp(lw_ref[b])
        rows = []
        for h in range(nh):
            s = s_ref[b, h]
            sa = -jnp.sum(s * kk_b[h:h + 1], axis=-1, keepdims=True)
            v_col = jnp.sum(jnp.where(eye, v_b[h:h + 1], 0.0), axis=-1, keepdims=True)
            s_new = s * dec_b[h:h + 1] + sa * ba_b[h:h + 1] + v_col * kh_b[h:h + 1]
            so_ref[b, h] = s_new
            y_col = jnp.sum(s_new * r_b[h:h + 1], axis=-1, keepdims=True)
            rows.append(jnp.sum(jnp.where(eye, y_col, 0.0), axis=0, keepdims=True))
        y = jnp.concatenate(rows, axis=0)
        mean = jnp.sum(y, axis=-1, keepdims=True) * inv
        dlt = y - mean
        var = jnp.sum(dlt * dlt, axis=-1, keepdims=True) * inv
        yn = dlt * lax.rsqrt(var + GN_EPS) * lnw_ref[...] + lnb_ref[...]
        bonus = jnp.sum(r_b * kh_b * rk_ref[...], axis=-1, keepdims=True) * v_b
        o_ref[b] = (yn + bonus) * g_ref[b]
        return carry

    lax.fori_loop(0, bb, body, 0)


def _rwkv_step(state, r, kh, v, kk, ba, lw, g, rk, lnw, lnb, bb):
    nb, nh = state.shape[0], state.shape[1]
    vec = pl.BlockSpec((bb, nh, HEAD), lambda i: (i, 0, 0))
    st = pl.BlockSpec((bb, nh, HEAD, HEAD), lambda i: (i, 0, 0, 0))
    par = _const_spec((nh, HEAD))
    vmem = 4 * bb * nh * HEAD * LANES * 4 + (16 << 20)
    return pl.pallas_call(
        _step_kernel,
        grid=(nb // bb,),
        in_specs=[st] + [vec] * 7 + [par] * 3,
        out_specs=[st, vec],
        out_shape=[jax.ShapeDtypeStruct(state.shape, F32),
                   jax.ShapeDtypeStruct((nb, nh, HEAD), F32)],
        compiler_params=_cparams(("arbitrary",), vmem),
        name="rwkv_step",
    )(state, r, kh, v, kk, ba, lw, g, rk, lnw, lnb)


def _alibi_slope(h, n_heads):
    return 2.0 ** (-8.0 * (h + 1) / n_heads)


def _attn_prompt_kernel(sink_ref, q_ref, kp_ref, kc_ref, vp_ref, vc_ref, o_ref):
    n = pl.program_id(1)
    nq = q_ref.shape[1] // HEAD
    group = nq // N_KV
    q = q_ref[...]
    k2 = jnp.concatenate([kp_ref[...], kc_ref[...]], axis=0).astype(BF16)
    v2 = jnp.concatenate([vp_ref[...], vc_ref[...]], axis=0).astype(BF16)
    qi = lax.broadcasted_iota(jnp.int32, (WINDOW, 2 * WINDOW), 0)
    si = lax.broadcasted_iota(jnp.int32, (WINDOW, 2 * WINDOW), 1)
    dist = qi + WINDOW - si
    first_key = jnp.where(n > 0, 0, WINDOW)
    valid = (dist >= 0) & (dist < WINDOW) & (si >= first_key)
    distf = dist.astype(F32)
    outs = []
    for h in range(nq):
        j = h // group
        qh = q[:, h * HEAD:(h + 1) * HEAD]
        s = _dot_nt(qh, k2[:, j * HEAD:(j + 1) * HEAD]) * SCALE - _alibi_slope(h, nq) * distf
        s = jnp.where(valid, s, -jnp.inf)
        sink = sink_ref[h]
        mx = jnp.maximum(jnp.max(s, axis=-1, keepdims=True), sink)
        p = jnp.exp(s - mx)
        den = jnp.sum(p, axis=-1, keepdims=True) + jnp.exp(sink - mx)
        outs.append(_dot(p, v2[:, j * HEAD:(j + 1) * HEAD]) / den)
    o_ref[...] = jnp.concatenate(outs, axis=1)


def _attn_prompt(z, sinks, nb, t, dq, q_blk, k_blk, v_blk):
    nblk = t // WINDOW
    dkv = N_KV * HEAD
    cur = lambda col: (lambda b, n: (b * nblk + n, col))
    prev = lambda col: (lambda b, n: (b * nblk + jnp.maximum(n - 1, 0), col))
    vmem = 4 * WINDOW * (dq * 2 + 4 * dkv) * 4 + (24 << 20)
    return pl.pallas_call(
        _attn_prompt_kernel,
        grid=(nb, nblk),
        in_specs=[pl.BlockSpec(memory_space=pltpu.SMEM),
                  pl.BlockSpec((WINDOW, dq), cur(q_blk)),
                  pl.BlockSpec((WINDOW, dkv), prev(k_blk)),
                  pl.BlockSpec((WINDOW, dkv), cur(k_blk)),
                  pl.BlockSpec((WINDOW, dkv), prev(v_blk)),
                  pl.BlockSpec((WINDOW, dkv), cur(v_blk))],
        out_specs=pl.BlockSpec((WINDOW, dq), lambda b, n: (b * nblk + n, 0)),
        out_shape=jax.ShapeDtypeStruct((nb * t, dq), F32),
        compiler_params=_cparams(("arbitrary", "arbitrary"), vmem),
        name="attn_prompt",
    )(sinks, z, z, z, z, z)


def _attn_sample_kernel(q_ref, kn_ref, vn_ref, ck_ref, cv_ref, sink_ref, slope_ref,
                        o_ref, ko_ref, vo_ref):
    bb, nq = q_ref.shape[0], q_ref.shape[1]
    group = nq // N_KV
    dkv = N_KV * HEAD
    hrow = lax.broadcasted_iota(jnp.int32, (nq, dkv), 0) // group
    hlane = lax.broadcasted_iota(jnp.int32, (nq, dkv), 1) >> 6
    own = hrow == hlane
    si = lax.broadcasted_iota(jnp.int32, (nq, WINDOW), 1)
    distf = (WINDOW - si).astype(F32)
    valid = si >= 1
    sink = sink_ref[...]
    slope = slope_ref[...]

    def body(b, carry):
        q = q_ref[b]
        q_bd = jnp.where(own, jnp.concatenate([q] * N_KV, axis=1), 0.0)
        ck = ck_ref[b]
        cv = cv_ref[b]
        kn = kn_ref[b]
        vn = vn_ref[b]
        s_c = _dot_nt(q_bd, ck) * SCALE - slope * distf
        s_c = jnp.where(valid, s_c, -jnp.inf)
        s_n = jnp.sum(q_bd.astype(BF16).astype(F32) * kn.astype(BF16).astype(F32),
                      axis=-1, keepdims=True) * SCALE
        mx = jnp.maximum(jnp.maximum(jnp.max(s_c, axis=-1, keepdims=True), s_n), sink)
        p_c = jnp.exp(s_c - mx)
        p_n = jnp.exp(s_n - mx)
        den = jnp.sum(p_c, axis=-1, keepdims=True) + p_n + jnp.exp(sink - mx)
        o_all = (_dot(p_c, cv) + p_n.astype(BF16).astype(F32) * vn.astype(BF16).astype(F32)) / den
        o_ref[b] = jnp.concatenate(
            [o_all[j * group:(j + 1) * group, j * HEAD:(j + 1) * HEAD] for j in range(N_KV)], axis=0)
        ko_ref[b] = jnp.concatenate([ck[1:], kn], axis=0)
        vo_ref[b] = jnp.concatenate([cv[1:], vn], axis=0)
        return carry

    lax.fori_loop(0, bb, body, 0)


def _attn_sample(q, kn, vn, ck, cv, sinks, slopes, bb):
    nb, nq = q.shape[0], q.shape[1]
    dkv = N_KV * HEAD
    cache = pl.BlockSpec((bb, WINDOW, dkv), lambda i: (i, 0, 0))
    new = pl.BlockSpec((bb, 1, dkv), lambda i: (i, 0, 0))
    qs = pl.BlockSpec((bb, nq, HEAD), lambda i: (i, 0, 0))
    vmem = 8 * bb * WINDOW * dkv * 4 + (16 << 20)
    return pl.pallas_call(
        _attn_sample_kernel,
        grid=(nb // bb,),
        in_specs=[qs, new, new, cache, cache, _const_spec((nq, 1)), _const_spec((nq, 1))],
        out_specs=[qs, cache, cache],
        out_shape=[jax.ShapeDtypeStruct((nb, nq, HEAD), F32),
                   jax.ShapeDtypeStruct(ck.shape, F32),
                   jax.ShapeDtypeStruct(cv.shape, F32)],
        compiler_params=_cparams(("arbitrary",), vmem),
        name="attn_sample",
    )(q, kn, vn, ck, cv, sinks, slopes)


def _out_proj_kernel(a_ref, b_ref, x_ref, w_ref, o_ref):
    d = a_ref.shape[1]
    o_ref[...] = (x_ref[...]
                  + jnp.dot(a_ref[...].astype(BF16), w_ref[0:d, :], preferred_element_type=F32)
                  + jnp.dot(b_ref[...].astype(BF16), w_ref[d:, :], preferred_element_type=F32))


def _out_proj(a, b, x, w, tm):
    m, d = a.shape
    dm = x.shape[1]
    vmem = 4 * tm * d * 4 + 4 * tm * dm * 4 + 2 * 2 * d * dm * 2 + (8 << 20)
    return pl.pallas_call(
        _out_proj_kernel,
        grid=(m // tm,),
        in_specs=[pl.BlockSpec((tm, d), lambda i: (i, 0)),
                  pl.BlockSpec((tm, d), lambda i: (i, 0)),
                  pl.BlockSpec((tm, dm), lambda i: (i, 0)),
                  _const_spec((2 * d, dm))],
        out_specs=pl.BlockSpec((tm, dm), lambda i: (i, 0)),
        out_shape=jax.ShapeDtypeStruct((m, dm), F32),
        compiler_params=_cparams(("arbitrary",), vmem),
        name="out_proj",
    )(a, b, x, w)


def _ffn_kernel(x_ref, g_ref, wg_ref, wu_ref, wd_ref, o_ref, h_ref, acc_ref):
    j = pl.program_id(1)

    @pl.when(j == 0)
    def _():
        h_ref[...] = _rms(x_ref[...], g_ref[...]).astype(BF16)
        acc_ref[...] = jnp.zeros_like(acc_ref)

    h = h_ref[...]
    gate = jnp.dot(h, wg_ref[...], preferred_element_type=F32)
    up = jnp.dot(h, wu_ref[...], preferred_element_type=F32)
    act = gate * _sigmoid(gate) * up
    acc_ref[...] += jnp.dot(act.astype(BF16), wd_ref[...], preferred_element_type=F32)

    @pl.when(j == pl.num_programs(1) - 1)
    def _():
        o_ref[...] = x_ref[...] + acc_ref[...]


def _ffn(x, gain, wg, wu, wd, tm, tf):
    m, dm = x.shape
    f = wg.shape[1]
    vmem = 4 * tm * dm * 4 + tm * dm * 2 + tm * dm * 4 + 6 * dm * tf * 2 + 4 * tm * tf * 4 + (6 << 20)
    return pl.pallas_call(
        _ffn_kernel,
        grid=(m // tm, f // tf),
        in_specs=[pl.BlockSpec((tm, dm), lambda i, j: (i, 0)),
                  _const_spec((1, dm)),
                  pl.BlockSpec((dm, tf), lambda i, j: (0, j)),
                  pl.BlockSpec((dm, tf), lambda i, j: (0, j)),
                  pl.BlockSpec((tf, dm), lambda i, j: (j, 0))],
        out_specs=pl.BlockSpec((tm, dm), lambda i, j: (i, 0)),
        out_shape=jax.ShapeDtypeStruct((m, dm), F32),
        scratch_shapes=[pltpu.VMEM((tm, dm), BF16), pltpu.VMEM((tm, dm), F32)],
        compiler_params=_cparams(("arbitrary", "arbitrary"), vmem),
        name="ffn",
    )(x, gain, wg, wu, wd)


def _ple_kernel(x_ref, p_ref, gn_ref, wgate_ref, wproj_ref, fn_ref, o_ref):
    x = x_ref[...]
    gate = _sigmoid(jnp.dot(_rms(x, gn_ref[...]).astype(BF16), wgate_ref[...],
                            preferred_element_type=F32))
    emb = jnp.dot(p_ref[...].astype(BF16), wproj_ref[...], preferred_element_type=F32)
    o_ref[...] = _rms(x + gate * emb, fn_ref[...])


def _ple_final(x, p, gn, wgate, wproj, fn, tm):
    m, dm = x.shape
    dp = p.shape[1]
    vmem = 4 * tm * dm * 4 + 2 * tm * dp * 4 + 2 * dm * dm * 2 + 2 * dp * dm * 2 + 4 * tm * dm * 4 + (6 << 20)
    return pl.pallas_call(
        _ple_kernel,
        grid=(m // tm,),
        in_specs=[pl.BlockSpec((tm, dm), lambda i: (i, 0)),
                  pl.BlockSpec((tm, dp), lambda i: (i, 0)),
                  _const_spec((1, dm)), _const_spec((dm, dm)), _const_spec((dp, dm)),
                  _const_spec((1, dm))],
        out_specs=pl.BlockSpec((tm, dm), lambda i: (i, 0)),
        out_shape=jax.ShapeDtypeStruct((m, dm), F32),
        compiler_params=_cparams(("arbitrary",), vmem),
        name="ple_final",
    )(x, p, gn, wgate, wproj, fn)


def _pad_cols(a, n):
    return jnp.pad(a, ((0, 0), (0, n - a.shape[1])))


def _layer(x, p_emb, lw_, final_norm, *, nb, t, state, shift_prev, caches):
    m, dm = x.shape
    d = lw_["d"]
    dq = lw_["dq"]
    dkv = N_KV * HEAD
    sample = caches is not None
    o_q, o_k, o_v, o_l = 3 * d, 3 * d + dq, 3 * d + dq + dkv, 3 * d + dq + 2 * dkv
    tm = min(m, 1024)
    z = _norm_matmul(x, lw_["norm_mix"], lw_["w_in"], tm, 1024)
    lora_blk = o_l // 512
    nh = d // HEAD
    if not sample:
        feats = _rwkv_prep_prompt(z, lw_["prep"], nb, t, d, 256, lora_blk)
        out_r, wkv = _rwkv_scan(*feats, lw_["r_k"], lw_["ln_w"], lw_["ln_b"], nb, t, d)
        y_a = _attn_prompt(z, lw_["sinks"], nb, t, dq, o_q // dq, o_k // dkv, o_v // dkv)
        z3 = z.reshape(nb, t, -1)
        shift_rows = z3[:, -1]
        k_new = z3[:, -WINDOW:, o_k:o_k + dkv].reshape(nb, WINDOW, N_KV, HEAD)
        v_new = z3[:, -WINDOW:, o_v:o_v + dkv].reshape(nb, WINDOW, N_KV, HEAD)
    else:
        prev_r = shift_prev[:, :3 * d]
        prev_l = jnp.concatenate(
            [_pad_cols(shift_prev[:, 3 * d:3 * d + LORA_W], LORA_PAD),
             _pad_cols(shift_prev[:, 3 * d + LORA_W:3 * d + LORA_W + LORA_A], LORA_PAD),
             shift_prev[:, 3 * d + LORA_W + LORA_A:]], axis=1)
        feats = _rwkv_prep_sample(z, prev_r, prev_l, lw_["prep"], d, lora_blk)
        feats3 = [f.reshape(m, nh, HEAD) for f in feats]
        wkv, out_r3 = _rwkv_step(state, *feats3, lw_["r_k"].reshape(nh, HEAD),
                                 lw_["ln_w"].reshape(nh, HEAD), lw_["ln_b"].reshape(nh, HEAD), 8)
        out_r = out_r3.reshape(m, d)
        ck, cv = caches
        q3 = z[:, o_q:o_q + dq].reshape(m, dq // HEAD, HEAD)
        kn = z[:, o_k:o_k + dkv].reshape(m, 1, dkv)
        vn = z[:, o_v:o_v + dkv].reshape(m, 1, dkv)
        y3, k_new, v_new = _attn_sample(q3, kn, vn, ck.reshape(m, WINDOW, dkv),
                                        cv.reshape(m, WINDOW, dkv), lw_["sinks"].reshape(-1, 1),
                                        lw_["slopes"], 8)
        y_a = y3.reshape(m, dq)
        k_new = k_new.reshape(m, WINDOW, N_KV, HEAD)
        v_new = v_new.reshape(m, WINDOW, N_KV, HEAD)
        shift_rows = z
    shift_new = jnp.concatenate(
        [shift_rows[:, :3 * d], shift_rows[:, o_l:o_l + LORA_W],
         shift_rows[:, o_l + LORA_PAD:o_l + LORA_PAD + LORA_A],
         shift_rows[:, o_l + 2 * LORA_PAD:o_l + 2 * LORA_PAD + LORA_G]], axis=1)
    tm2 = min(m, 512)
    x1 = _out_proj(out_r, y_a, x, lw_["w_out"], tm2)
    x2 = _ffn(x1, lw_["norm_ffn"], lw_["w_gate"], lw_["w_up"], lw_["w_down"], tm2, 512)
    y = _ple_final(x2, p_emb, lw_["norm_ple"], lw_["ple_gate"], lw_["ple_proj"], final_norm, tm2)
    return y, wkv, shift_new, k_new, v_new


def kernel(x_prompt, x_sample, state_wkv, state_shift, cache_k, cache_v, p_prompt, p_sample, norm_mix, w_in, mu_shift, rwkv_w0, rwkv_w2, rwkv_a0, rwkv_a2, rwkv_g2, rwkv_k_k, rwkv_k_a, rwkv_r_k, rwkv_ln_w, rwkv_ln_b, attn_sinks, w_out, norm_ffn, w_gate, w_up, w_down, norm_ple, ple_gate, ple_proj, final_norm):
    depth = w_in.shape[0]
    assert depth == 1, "single-layer step"
    bp, t, dm = x_prompt.shape
    bs = x_sample.shape[0]
    assert x_sample.shape[1] == 1
    d = rwkv_w0.shape[1]
    dq = attn_sinks.shape[1] * HEAD
    dkv = N_KV * HEAD
    i = 0
    row = lambda a: a.reshape(1, -1)
    wi = w_in[i]
    rp = 3 * d + LORA_W + LORA_A + LORA_G
    w_in_p = jnp.concatenate(
        [wi[:, :3 * d], wi[:, rp:],
         _pad_cols(wi[:, 3 * d:3 * d + LORA_W], LORA_PAD),
         _pad_cols(wi[:, 3 * d + LORA_W:3 * d + LORA_W + LORA_A], LORA_PAD),
         wi[:, 3 * d + LORA_W + LORA_A:rp]], axis=1).astype(BF16)
    mu = row(mu_shift[i])
    mu_r = mu[:, :3 * d]
    mu_l = jnp.concatenate(
        [_pad_cols(mu[:, 3 * d:3 * d + LORA_W], LORA_PAD),
         _pad_cols(mu[:, 3 * d + LORA_W:3 * d + LORA_W + LORA_A], LORA_PAD),
         mu[:, 3 * d + LORA_W + LORA_A:]], axis=1)
    pad_rows = lambda a: jnp.pad(a, ((0, LORA_PAD - a.shape[0]), (0, 0))).astype(BF16)
    nq = dq // HEAD
    lw_ = {
        "d": d, "dq": dq,
        "norm_mix": row(norm_mix[i]), "w_in": w_in_p,
        "prep": (mu_r, mu_l, row(rwkv_w0[i]), pad_rows(rwkv_w2[i]), row(rwkv_a0[i]),
                 pad_rows(rwkv_a2[i]), rwkv_g2[i].astype(BF16), row(rwkv_k_k[i]), row(rwkv_k_a[i])),
        "r_k": row(rwkv_r_k[i]), "ln_w": row(rwkv_ln_w[i]), "ln_b": row(rwkv_ln_b[i]),
        "sinks": attn_sinks[i],
        "slopes": (2.0 ** (-8.0 * jnp.arange(1, nq + 1, dtype=F32) / nq)).reshape(nq, 1),
        "w_out": w_out[i].astype(BF16), "norm_ffn": row(norm_ffn[i]),
        "w_gate": w_gate[i].astype(BF16), "w_up": w_up[i].astype(BF16),
        "w_down": w_down[i].astype(BF16), "norm_ple": row(norm_ple[i]),
        "ple_gate": ple_gate[i].astype(BF16), "ple_proj": ple_proj[i].astype(BF16),
    }
    fn = row(final_norm)
    yp, wkv_p, shift_p, k_p, v_p = _layer(
        x_prompt.reshape(bp * t, dm), p_prompt[i].reshape(bp * t, -1), lw_, fn,
        nb=bp, t=t, state=None, shift_prev=None, caches=None)
    ys, wkv_s, shift_s, k_s, v_s = _layer(
        x_sample.reshape(bs, dm), p_sample[i].reshape(bs, -1), lw_, fn,
        nb=bs, t=1, state=state_wkv[i], shift_prev=state_shift[i],
        caches=(cache_k[i], cache_v[i]))
    return (yp.reshape(bp, t, dm), ys.reshape(bs, 1, dm),
            wkv_p[None], shift_p[None], k_p[None], v_p[None],
            wkv_s[None], shift_s[None], k_s[None], v_s[None])
```

```python
import functools
import math

import jax
import jax.numpy as jnp
from jax import lax
from jax.experimental import pallas as pl
from jax.experimental.pallas import tpu as pltpu

F32 = jnp.float32
BF16 = jnp.bfloat16

HEAD = 64
LANES = 128
N_KV = 4
WINDOW = 128
LORA_W = 96
LORA_A = 96
LORA_G = 256
LORA_PAD = 128
EPS = 1e-6
GN_EPS = 64e-5
SCALE = HEAD ** -0.5
CHUNK = 64
DECAY_SCALE = math.exp(-0.5)
VMEM_CAP = 60 * 1024 * 1024


def _cparams(sem, vmem_bytes):
    return pltpu.CompilerParams(dimension_semantics=sem,
                                vmem_limit_bytes=int(min(VMEM_CAP, vmem_bytes)))


def _dot(a, b):
    return jnp.dot(a.astype(BF16), b.astype(BF16), preferred_element_type=F32)


def _dot_nt(a, b):
    return lax.dot_general(a.astype(BF16), b.astype(BF16), (((1,), (1,)), ((), ())),
                           preferred_element_type=F32)


def _dot_tn(a, b):
    return lax.dot_general(a.astype(BF16), b.astype(BF16), (((0,), (0,)), ((), ())),
                           preferred_element_type=F32)


def _split2(x):
    hi = x.astype(BF16)
    lo = (x - hi.astype(F32)).astype(BF16)
    return hi, lo


def _split3(x):
    hi = x.astype(BF16)
    r1 = x - hi.astype(F32)
    mid = r1.astype(BF16)
    lo = (r1 - mid.astype(F32)).astype(BF16)
    return hi, mid, lo


def _head_ones():
    i = lax.broadcasted_iota(jnp.int32, (LANES, LANES), 0) >> 6
    j = lax.broadcasted_iota(jnp.int32, (LANES, LANES), 1) >> 6
    return jnp.where(i == j, 1.0, 0.0).astype(BF16)


def _segsum(x, ones_bd):
    hi, lo = _split2(x)
    return (jnp.dot(hi, ones_bd, preferred_element_type=F32)
            + jnp.dot(lo, ones_bd, preferred_element_type=F32))


def _rms(x, g):
    ms = jnp.mean(x * x, axis=-1, keepdims=True)
    return x * lax.rsqrt(ms + EPS) * g


def _sigmoid(x):
    return 1.0 / (1.0 + jnp.exp(-x))


def _norm_mm_kernel(x_ref, g_ref, w_ref, o_ref, h_ref):
    @pl.when(pl.program_id(1) == 0)
    def _():
        h_ref[...] = _rms(x_ref[...], g_ref[...]).astype(BF16)

    o_ref[...] = jnp.dot(h_ref[...], w_ref[...], preferred_element_type=F32)


def _norm_matmul(x, gain, w, tm, tn):
    m, k = x.shape
    n = w.shape[1]
    vmem = 2 * tm * k * 4 + 2 * k * tn * 2 + 2 * tm * tn * 4 + tm * k * 2 + (4 << 20)
    return pl.pallas_call(
        _norm_mm_kernel,
        grid=(m // tm, n // tn),
        in_specs=[pl.BlockSpec((tm, k), lambda i, j: (i, 0)),
                  pl.BlockSpec((1, k), lambda i, j: (0, 0)),
                  pl.BlockSpec((k, tn), lambda i, j: (0, j))],
        out_specs=pl.BlockSpec((tm, tn), lambda i, j: (i, j)),
        out_shape=jax.ShapeDtypeStruct((m, n), F32),
        scratch_shapes=[pltpu.VMEM((tm, k), BF16)],
        compiler_params=_cparams(("arbitrary", "arbitrary"), vmem),
        name="norm_in_proj",
    )(x, gain, w)


def _prep_core(zr, zpr, zl, zpl, mur_ref, mul_ref, w0_ref, w2_ref, a0_ref, a2_ref, g2_ref,
               kk_ref, ka_ref, out_refs):
    r_o, kh_o, v_o, kk_o, ba_o, lw_o, g_o = out_refs
    d = r_o.shape[-1]
    xr = zr + (zpr - zr) * mur_ref[...]
    xl = zl + (zpl - zl) * mul_ref[...]
    r = xr[:, 0:d]
    k = xr[:, d:2 * d]
    v = xr[:, 2 * d:3 * d]
    xw = xl[:, 0:LORA_PAD]
    xa = xl[:, LORA_PAD:2 * LORA_PAD]
    xg = xl[:, 2 * LORA_PAD:2 * LORA_PAD + LORA_G]
    u = w0_ref[...] + jnp.dot(jnp.tanh(xw).astype(BF16), w2_ref[...], preferred_element_type=F32)
    lw = -DECAY_SCALE * _sigmoid(u)
    a = _sigmoid(a0_ref[...] + jnp.dot(xa.astype(BF16), a2_ref[...], preferred_element_type=F32))
    g = jnp.dot(_sigmoid(xg).astype(BF16), g2_ref[...], preferred_element_type=F32)
    kk = k * kk_ref[...]
    ones_bd = _head_ones()
    kk2 = kk * kk
    ss = jnp.concatenate([_segsum(kk2[:, t * LANES:(t + 1) * LANES], ones_bd)
                          for t in range(d // LANES)], axis=1)
    kk = kk * lax.rsqrt(jnp.maximum(ss, 1e-24))
    kh = k * (1.0 + (a - 1.0) * ka_ref[...])
    r_o[...] = r.astype(r_o.dtype)
    kh_o[...] = kh.astype(kh_o.dtype)
    v_o[...] = v.astype(v_o.dtype)
    kk_o[...] = kk.astype(kk_o.dtype)
    ba_o[...] = (kk * a).astype(ba_o.dtype)
    lw_o[...] = lw
    g_o[...] = g.astype(g_o.dtype)


def _prep_prompt_kernel(zr_ref, zl_ref, mur, mul, w0, w2, a0, a2, g2, kkp, kap,
                        r_o, kh_o, v_o, kk_o, ba_o, lw_o, g_o, cr_ref, cl_ref):
    @pl.when(pl.program_id(1) == 0)
    def _():
        cr_ref[...] = jnp.zeros_like(cr_ref)
        cl_ref[...] = jnp.zeros_like(cl_ref)

    zr = zr_ref[...]
    zl = zl_ref[...]
    tt = zr.shape[0]
    first = lax.broadcasted_iota(jnp.int32, (tt, 1), 0) == 0
    zpr = jnp.where(first, cr_ref[0:1, :], pltpu.roll(zr, 1, 0))
    zpl = jnp.where(first, cl_ref[0:1, :], pltpu.roll(zl, 1, 0))
    cr_ref[0:1, :] = zr[tt - 1:tt, :]
    cl_ref[0:1, :] = zl[tt - 1:tt, :]
    _prep_core(zr, zpr, zl, zpl, mur, mul, w0, w2, a0, a2, g2, kkp, kap,
               (r_o, kh_o, v_o, kk_o, ba_o, lw_o, g_o))


def _prep_sample_kernel(zr_ref, zl_ref, pr_ref, pl_ref, mur, mul, w0, w2, a0, a2, g2, kkp, kap,
                        r_o, kh_o, v_o, kk_o, ba_o, lw_o, g_o):
    _prep_core(zr_ref[...], pr_ref[...], zl_ref[...], pl_ref[...], mur, mul, w0, w2, a0, a2, g2,
               kkp, kap, (r_o, kh_o, v_o, kk_o, ba_o, lw_o, g_o))


def _const_spec(shape):
    return pl.BlockSpec(shape, lambda *_: (0,) * len(shape))


def _prep_param_specs(d):
    return [_const_spec((1, 3 * d)), _const_spec((1, 512)), _const_spec((1, d)),
            _const_spec((LORA_PAD, d)), _const_spec((1, d)), _const_spec((LORA_PAD, d)),
            _const_spec((LORA_G, d)), _const_spec((1, d)), _const_spec((1, d))]


def _rwkv_prep_prompt(z, params, nb, t, d, tt, lora_blk):
    m = nb * t
    nt = t // tt
    row = lambda b, i: (b * nt + i, 0)
    act = jax.ShapeDtypeStruct((m, d), BF16)
    out_shape = [act] * 5 + [jax.ShapeDtypeStruct((m, d), F32), act]
    vmem = 2 * tt * (3 * d + 512) * 4 + 14 * tt * d * 4 + 24 * tt * d * 4 + (8 << 20)
    return pl.pallas_call(
        _prep_prompt_kernel,
        grid=(nb, nt),
        in_specs=[pl.BlockSpec((tt, 3 * d), row),
                  pl.BlockSpec((tt, 512), lambda b, i: (b * nt + i, lora_blk))]
                 + _prep_param_specs(d),
        out_specs=[pl.BlockSpec((tt, d), row)] * 7,
        out_shape=out_shape,
        scratch_shapes=[pltpu.VMEM((8, 3 * d), F32), pltpu.VMEM((8, 512), F32)],
        compiler_params=_cparams(("arbitrary", "arbitrary"), vmem),
        name="rwkv_prep_prompt",
    )(z, z, *params)


def _rwkv_prep_sample(z, prev_r, prev_l, params, d, lora_blk):
    m = z.shape[0]
    out = jax.ShapeDtypeStruct((m, d), F32)
    vmem = 4 * m * (3 * d + 512) * 4 + 14 * m * d * 4 + 24 * m * d * 4 + (8 << 20)
    return pl.pallas_call(
        _prep_sample_kernel,
        grid=(1,),
        in_specs=[pl.BlockSpec((m, 3 * d), lambda i: (0, 0)),
                  pl.BlockSpec((m, 512), lambda i: (0, lora_blk)),
                  _const_spec((m, 3 * d)), _const_spec((m, 512))]
                 + _prep_param_specs(d),
        out_specs=[pl.BlockSpec((m, d), lambda i: (0, 0))] * 7,
        out_shape=[out] * 7,
        compiler_params=_cparams(("arbitrary",), vmem),
        name="rwkv_prep_sample",
    )(z, z, prev_r, prev_l, *params)


def _rwkv_post(y, r, kh, v, g, rk, lnw, lnb, ones_bd, n_tiles):
    rows = y.shape[0] // n_tiles
    par = lambda a: jnp.concatenate(
        [jnp.broadcast_to(a[:, p * LANES:(p + 1) * LANES], (rows, LANES)) for p in range(n_tiles)], axis=0)
    inv = 1.0 / HEAD
    mean = _segsum(y, ones_bd) * inv
    dlt = y - mean
    var = _segsum(dlt * dlt, ones_bd) * inv
    yn = dlt * lax.rsqrt(var + GN_EPS) * par(lnw) + par(lnb)
    bonus = _segsum(r * kh * par(rk), ones_bd) * v
    return (yn + bonus) * g


def _pair_diag(x, lane_lo):
    return jnp.concatenate([jnp.where(lane_lo, x, 0.0), jnp.where(lane_lo, 0.0, x)], axis=0)


def _scan_kernel(r_ref, kh_ref, v_ref, kk_ref, ba_ref, lw_ref, g_ref, rk_ref, lnw_ref, lnb_ref,
                 o_ref, st_ref, s_scr):
    c = pl.program_id(1)
    n_pairs = s_scr.shape[0]
    C = CHUNK

    @pl.when(c == 0)
    def _():
        s_scr[...] = jnp.zeros_like(s_scr)

    row = lax.broadcasted_iota(jnp.int32, (C, LANES), 0)
    lane = lax.broadcasted_iota(jnp.int32, (C, LANES), 1)
    src = lane & (HEAD - 1)
    strict = src < row
    incl = src <= row
    eye_w = jnp.where(src == row, 1.0, 0.0)
    lane_lo = lane < HEAD
    sq_r = lax.broadcasted_iota(jnp.int32, (LANES, LANES), 0) >> 6
    sq_c = lax.broadcasted_iota(jnp.int32, (LANES, LANES), 1) >> 6
    same_head = sq_r == sq_c
    ones_bd = jnp.where(same_head, 1.0, 0.0).astype(BF16)
    tri = jnp.where(lax.broadcasted_iota(jnp.int32, (C, C), 1)
                    <= lax.broadcasted_iota(jnp.int32, (C, C), 0), 1.0, 0.0).astype(BF16)

    pairs = range(n_pairs)
    tile = lambda x, p: x[:, p * LANES:(p + 1) * LANES]
    rows_of = lambda x: jnp.concatenate([tile(x, p) for p in pairs], axis=0)
    lanes_of = lambda x: jnp.concatenate([x[p * C:(p + 1) * C] for p in pairs], axis=1)
    pd = lambda x: _pair_diag(x, lane_lo)

    f32 = lambda ref: ref[...].astype(F32)
    r, kh, v = f32(r_ref), f32(kh_ref), f32(v_ref)
    kk, ba, lw = f32(kk_ref), f32(ba_ref), lw_ref[...]
    l_hi, l_mid, l_lo = _split3(lw)
    cs = (jnp.dot(tri, l_hi, preferred_element_type=F32)
          + jnp.dot(tri, l_mid, preferred_element_type=F32)
          + jnp.dot(tri, l_lo, preferred_element_type=F32))
    c_end = cs[C - 1:C, :]
    e_neg = jnp.exp(-cs)
    e_end = jnp.exp(c_end - cs)
    r_t = r * jnp.exp(cs)
    k_x = kk * jnp.exp(cs - lw)
    k_t = kh * e_neg
    b_t = ba * e_neg
    k_e = kh * e_end
    b_e = ba * e_end
    g_end = jnp.exp(c_end)

    gram = [_dot_nt(jnp.concatenate([tile(k_x, p), tile(r_t, p)], axis=0),
                    jnp.concatenate([pd(tile(k_t, p)), pd(tile(b_t, p))], axis=0)) for p in pairs]
    a_kk = [jnp.where(strict, gm[0:C, 0:LANES], 0.0) for gm in gram]
    a_kb = [jnp.where(strict, gm[0:C, LANES:2 * LANES], 0.0) for gm in gram]
    a_rk = [jnp.where(incl, gm[C:2 * C, 0:LANES], 0.0) for gm in gram]
    a_rb = [jnp.where(incl, gm[C:2 * C, LANES:2 * LANES], 0.0) for gm in gram]

    pw = [_dot(a, pd(a)) for a in a_kb]
    t_inv = [eye_w - a for a in a_kb]
    n_lvl = int(math.log2(C)) - 1
    for lvl in range(n_lvl):
        last = lvl == n_lvl - 1
        prod = [_dot(t_inv[p] if last else jnp.concatenate([t_inv[p], pw[p]], axis=0), pd(pw[p]))
                for p in pairs]
        t_inv = [t_inv[p] + prod[p][0:C] for p in pairs]
        if not last:
            pw = [prod[p][C:2 * C] for p in pairs]

    s_bd = [s_scr[p] for p in pairs]
    v_bd = [pd(tile(v, p)) for p in pairs]
    w_mat = [_dot_nt(tile(k_x, p), s_bd[p]) + _dot(a_kk[p], v_bd[p]) for p in pairs]
    u = [_dot(t_inv[p], pd(w_mat[p])) for p in pairs]
    y = [_dot_nt(tile(r_t, p), s_bd[p]) + _dot(a_rk[p], v_bd[p]) - _dot(a_rb[p], pd(u[p]))
         for p in pairs]
    for p in pairs:
        upd = _dot_tn(jnp.concatenate([tile(v, p), u[p]], axis=0),
                      jnp.concatenate([tile(k_e, p), -tile(b_e, p)], axis=0))
        s_scr[p] = s_bd[p] * tile(g_end, p) + jnp.where(same_head, upd, 0.0)

    y_rows = jnp.concatenate(y, axis=0)
    out = _rwkv_post(y_rows, rows_of(r), rows_of(kh), rows_of(v), rows_of(f32(g_ref)),
                     rk_ref[...], lnw_ref[...], lnb_ref[...], ones_bd, n_pairs)
    o_ref[...] = lanes_of(out)

    @pl.when(c == pl.num_programs(1) - 1)
    def _():
        for p in range(n_pairs):
            s = s_scr[p]
            st_ref[0, 2 * p] = s[0:HEAD, 0:HEAD]
            st_ref[0, 2 * p + 1] = s[HEAD:2 * HEAD, HEAD:2 * HEAD]


def _rwkv_scan(r, kh, v, kk, ba, lw, g, rk, lnw, lnb, nb, t, d):
    nc = t // CHUNK
    nh = d // HEAD
    blk = pl.BlockSpec((CHUNK, d), lambda b, c: (b * nc + c, 0))
    par = _const_spec((1, d))
    vmem = 16 * CHUNK * d * 4 + 3 * (d // LANES) * LANES * LANES * 4 + (16 << 20)
    return pl.pallas_call(
        _scan_kernel,
        grid=(nb, nc),
        in_specs=[blk] * 7 + [par] * 3,
        out_specs=[blk, pl.BlockSpec((1, nh, HEAD, HEAD), lambda b, c: (b, 0, 0, 0))],
        out_shape=[jax.ShapeDtypeStruct((nb * t, d), F32),
                   jax.ShapeDtypeStruct((nb, nh, HEAD, HEAD), F32)],
        scratch_shapes=[pltpu.VMEM((d // LANES, LANES, LANES), F32)],
        compiler_params=_cparams(("arbitrary", "arbitrary"), vmem),
        name="rwkv_scan",
    )(r, kh, v, kk, ba, lw, g, rk, lnw, lnb)


def _step_kernel(s_ref, r_ref, kh_ref, v_ref, kk_ref, ba_ref, lw_ref, g_ref, rk_ref, lnw_ref,
                 lnb_ref, so_ref, o_ref):
    bb, nh = r_ref.shape[0], r_ref.shape[1]
    eye = (lax.broadcasted_iota(jnp.int32, (HEAD, HEAD), 0)
           == lax.broadcasted_iota(jnp.int32, (HEAD, HEAD), 1))
    inv = 1.0 / HEAD

    def body(b, carry):
        r_b, kh_b, v_b = r_ref[b], kh_ref[b], v_ref[b]
        kk_b, ba_b, dec_b = kk_ref[b], ba_ref[b], jnp.exp(lw_ref[b])
        rows = []
        for h in range(nh):
            s = s_ref[b, h]
            sa = -jnp.sum(s * kk_b[h:h + 1], axis=-1, keepdims=True)
            v_col = jnp.sum(jnp.where(eye, v_b[h:h + 1], 0.0), axis=-1, keepdims=True)
            s_new = s * dec_b[h:h + 1] + sa * ba_b[h:h + 1] + v_col * kh_b[h:h + 1]
            so_ref[b, h] = s_new
            y_col = jnp.sum(s_new * r_b[h:h + 1], axis=-1, keepdims=True)
            rows.append(jnp.sum(jnp.where(eye, y_col, 0.0), axis=0, keepdims=True))
        y = jnp.concatenate(rows, axis=0)
        mean = jnp.sum(y, axis=-1, keepdims=True) * inv
        dlt = y - mean
        var = jnp.sum(dlt * dlt, axis=-1, keepdims=True) * inv
        yn = dlt * lax.rsqrt(var + GN_EPS) * lnw_ref[...] + lnb_ref[...]
        bonus = jnp.sum(r_b * kh_b * rk_ref[...], axis=-1, keepdims=True) * v_b
        o_ref[b] = (yn + bonus) * g_ref[b]
        return carry

    lax.fori_loop(0, bb, body, 0)


def _rwkv_step(state, r, kh, v, kk, ba, lw, g, rk, lnw, lnb, bb):
    nb, nh = state.shape[0], state.shape[1]
    vec = pl.BlockSpec((bb, nh, HEAD), lambda i: (i, 0, 0))
    st = pl.BlockSpec((bb, nh, HEAD, HEAD), lambda i: (i, 0, 0, 0))
    par = _const_spec((nh, HEAD))
    vmem = 4 * bb * nh * HEAD * LANES * 4 + (16 << 20)
    return pl.pallas_call(
        _step_kernel,
        grid=(nb // bb,),
        in_specs=[st] + [vec] * 7 + [par] * 3,
        out_specs=[st, vec],
        out_shape=[jax.ShapeDtypeStruct(state.shape, F32),
                   jax.ShapeDtypeStruct((nb, nh, HEAD), F32)],
        compiler_params=_cparams(("arbitrary",), vmem),
        name="rwkv_step",
    )(state, r, kh, v, kk, ba, lw, g, rk, lnw, lnb)


def _alibi_slope(h, n_heads):
    return 2.0 ** (-8.0 * (h + 1) / n_heads)


def _attn_prompt_kernel(sink_ref, q_ref, kp_ref, kc_ref, vp_ref, vc_ref, o_ref):
    n = pl.program_id(1)
    nq = q_ref.shape[1] // HEAD
    group = nq // N_KV
    q = q_ref[...]
    k2 = jnp.concatenate([kp_ref[...], kc_ref[...]], axis=0).astype(BF16)
    v2 = jnp.concatenate([vp_ref[...], vc_ref[...]], axis=0).astype(BF16)
    qi = lax.broadcasted_iota(jnp.int32, (WINDOW, 2 * WINDOW), 0)
    si = lax.broadcasted_iota(jnp.int32, (WINDOW, 2 * WINDOW), 1)
    dist = qi + WINDOW - si
    first_key = jnp.where(n > 0, 0, WINDOW)
    valid = (dist >= 0) & (dist < WINDOW) & (si >= first_key)
    distf = dist.astype(F32)
    outs = []
    for h in range(nq):
        j = h // group
        qh = q[:, h * HEAD:(h + 1) * HEAD]
        s = _dot_nt(qh, k2[:, j * HEAD:(j + 1) * HEAD]) * SCALE - _alibi_slope(h, nq) * distf
        s = jnp.where(valid, s, -jnp.inf)
        sink = sink_ref[h]
        mx = jnp.maximum(jnp.max(s, axis=-1, keepdims=True), sink)
        p = jnp.exp(s - mx)
        den = jnp.sum(p, axis=-1, keepdims=True) + jnp.exp(sink - mx)
        outs.append(_dot(p, v2[:, j * HEAD:(j + 1) * HEAD]) / den)
    o_ref[...] = jnp.concatenate(outs, axis=1)


def _attn_prompt(z, sinks, nb, t, dq, q_blk, k_blk, v_blk):
    nblk = t // WINDOW
    dkv = N_KV * HEAD
    cur = lambda col: (lambda b, n: (b * nblk + n, col))
    prev = lambda col: (lambda b, n: (b * nblk + jnp.maximum(n - 1, 0), col))
    vmem = 4 * WINDOW * (dq * 2 + 4 * dkv) * 4 + (24 << 20)
    return pl.pallas_call(
        _attn_prompt_kernel,
        grid=(nb, nblk),
        in_specs=[pl.BlockSpec(memory_space=pltpu.SMEM),
                  pl.BlockSpec((WINDOW, dq), cur(q_blk)),
                  pl.BlockSpec((WINDOW, dkv), prev(k_blk)),
                  pl.BlockSpec((WINDOW, dkv), cur(k_blk)),
                  pl.BlockSpec((WINDOW, dkv), prev(v_blk)),
                  pl.BlockSpec((WINDOW, dkv), cur(v_blk))],
        out_specs=pl.BlockSpec((WINDOW, dq), lambda b, n: (b * nblk + n, 0)),
        out_shape=jax.ShapeDtypeStruct((nb * t, dq), F32),
        compiler_params=_cparams(("arbitrary", "arbitrary"), vmem),
        name="attn_prompt",
    )(sinks, z, z, z, z, z)


def _attn_sample_kernel(q_ref, kn_ref, vn_ref, ck_ref, cv_ref, sink_ref, slope_ref,
                        o_ref, ko_ref, vo_ref):
    bb, nq = q_ref.shape[0], q_ref.shape[1]
    group = nq // N_KV
    si = lax.broadcasted_iota(jnp.int32, (group, WINDOW), 1)
    distf = (WINDOW - si).astype(F32)
    valid = si >= 1
    sink_all = sink_ref[...]
    slope_all = slope_ref[...]

    def body(b, carry):
        q = q_ref[b]
        outs = []
        for j in range(N_KV):
            rows = slice(j * group, (j + 1) * group)
            qj = q[rows]
            ck = ck_ref[b, :, j, :]
            cv = cv_ref[b, :, j, :]
            kn = kn_ref[b, :, j, :]
            vn = vn_ref[b, :, j, :]
            sink = sink_all[rows]
            s_c = _dot_nt(qj, ck) * SCALE - slope_all[rows] * distf
            s_c = jnp.where(valid, s_c, -jnp.inf)
            s_n = jnp.sum(qj * kn, axis=-1, keepdims=True) * SCALE
            mx = jnp.maximum(jnp.maximum(jnp.max(s_c, axis=-1, keepdims=True), s_n), sink)
            p_c = jnp.exp(s_c - mx)
            p_n = jnp.exp(s_n - mx)
            den = jnp.sum(p_c, axis=-1, keepdims=True) + p_n + jnp.exp(sink - mx)
            outs.append((_dot(p_c, cv) + p_n * vn) / den)
            ko_ref[b, :, j, :] = jnp.concatenate([ck[1:], kn], axis=0)
            vo_ref[b, :, j, :] = jnp.concatenate([cv[1:], vn], axis=0)
        o_ref[b] = jnp.concatenate(outs, axis=0)
        return carry

    lax.fori_loop(0, bb, body, 0)


def _attn_sample(q, kn, vn, ck, cv, sinks, slopes, bb):
    nb, nq = q.shape[0], q.shape[1]
    cache = pl.BlockSpec((bb, WINDOW, N_KV, HEAD), lambda i: (i, 0, 0, 0))
    new = pl.BlockSpec((bb, 1, N_KV, HEAD), lambda i: (i, 0, 0, 0))
    qs = pl.BlockSpec((bb, nq, HEAD), lambda i: (i, 0, 0))
    vmem = 8 * bb * WINDOW * 8 * LANES * 4 + (8 << 20)
    return pl.pallas_call(
        _attn_sample_kernel,
        grid=(nb // bb,),
        in_specs=[qs, new, new, cache, cache, _const_spec((nq, 1)), _const_spec((nq, 1))],
        out_specs=[qs, cache, cache],
        out_shape=[jax.ShapeDtypeStruct((nb, nq, HEAD), F32),
                   jax.ShapeDtypeStruct(ck.shape, F32),
                   jax.ShapeDtypeStruct(cv.shape, F32)],
        compiler_params=_cparams(("arbitrary",), vmem),
        name="attn_sample",
    )(q, kn, vn, ck, cv, sinks, slopes)


def _out_proj_kernel(a_ref, b_ref, x_ref, w_ref, o_ref):
    d = a_ref.shape[1]
    o_ref[...] = (x_ref[...]
                  + jnp.dot(a_ref[...].astype(BF16), w_ref[0:d, :], preferred_element_type=F32)
                  + jnp.dot(b_ref[...].astype(BF16), w_ref[d:, :], preferred_element_type=F32))


def _out_proj(a, b, x, w, tm):
    m, d = a.shape
    dm = x.shape[1]
    vmem = 4 * tm * d * 4 + 4 * tm * dm * 4 + 2 * 2 * d * dm * 2 + (8 << 20)
    return pl.pallas_call(
        _out_proj_kernel,
        grid=(m // tm,),
        in_specs=[pl.BlockSpec((tm, d), lambda i: (i, 0)),
                  pl.BlockSpec((tm, d), lambda i: (i, 0)),
                  pl.BlockSpec((tm, dm), lambda i: (i, 0)),
                  _const_spec((2 * d, dm))],
        out_specs=pl.BlockSpec((tm, dm), lambda i: (i, 0)),
        out_shape=jax.ShapeDtypeStruct((m, dm), F32),
        compiler_params=_cparams(("arbitrary",), vmem),
        name="out_proj",
    )(a, b, x, w)


def _ffn_kernel(x_ref, g_ref, wg_ref, wu_ref, wd_ref, o_ref, h_ref, acc_ref):
    j = pl.program_id(1)

    @pl.when(j == 0)
    def _():
        h_ref[...] = _rms(x_ref[...], g_ref[...]).astype(BF16)
        acc_ref[...] = jnp.zeros_like(acc_ref)

    h = h_ref[...]
    gate = jnp.dot(h, wg_ref[...], preferred_element_type=F32)
    up = jnp.dot(h, wu_ref[...], preferred_element_type=F32)
    act = gate * _sigmoid(gate) * up
    acc_ref[...] += jnp.dot(act.astype(BF16), wd_ref[...], preferred_element_type=F32)

    @pl.when(j == pl.num_programs(1) - 1)
    def _():
        o_ref[...] = x_ref[...] + acc_ref[...]


def _ffn(x, gain, wg, wu, wd, tm, tf):
    m, dm = x.shape
    f = wg.shape[1]
    vmem = 4 * tm * dm * 4 + tm * dm * 2 + tm * dm * 4 + 6 * dm * tf * 2 + 4 * tm * tf * 4 + (6 << 20)
    return pl.pallas_call(
        _ffn_kernel,
        grid=(m // tm, f // tf),
        in_specs=[pl.BlockSpec((tm, dm), lambda i, j: (i, 0)),
                  _const_spec((1, dm)),
                  pl.BlockSpec((dm, tf), lambda i, j: (0, j)),
                  pl.BlockSpec((dm, tf), lambda i, j: (0, j)),
                  pl.BlockSpec((tf, dm), lambda i, j: (j, 0))],
        out_specs=pl.BlockSpec((tm, dm), lambda i, j: (i, 0)),
        out_shape=jax.ShapeDtypeStruct((m, dm), F32),
        scratch_shapes=[pltpu.VMEM((tm, dm), BF16), pltpu.VMEM((tm, dm), F32)],
        compiler_params=_cparams(("arbitrary", "arbitrary"), vmem),
        name="ffn",
    )(x, gain, wg, wu, wd)


def _ple_kernel(x_ref, p_ref, gn_ref, wgate_ref, wproj_ref, fn_ref, o_ref):
    x = x_ref[...]
    gate = _sigmoid(jnp.dot(_rms(x, gn_ref[...]).astype(BF16), wgate_ref[...],
                            preferred_element_type=F32))
    emb = jnp.dot(p_ref[...].astype(BF16), wproj_ref[...], preferred_element_type=F32)
    o_ref[...] = _rms(x + gate * emb, fn_ref[...])


def _ple_final(x, p, gn, wgate, wproj, fn, tm):
    m, dm = x.shape
    dp = p.shape[1]
    vmem = 4 * tm * dm * 4 + 2 * tm * dp * 4 + 2 * dm * dm * 2 + 2 * dp * dm * 2 + 4 * tm * dm * 4 + (6 << 20)
    return pl.pallas_call(
        _ple_kernel,
        grid=(m // tm,),
        in_specs=[pl.BlockSpec((tm, dm), lambda i: (i, 0)),
                  pl.BlockSpec((tm, dp), lambda i: (i, 0)),
                  _const_spec((1, dm)), _const_spec((dm, dm)), _const_spec((dp, dm)),
                  _const_spec((1, dm))],
        out_specs=pl.BlockSpec((tm, dm), lambda i: (i, 0)),
        out_shape=jax.ShapeDtypeStruct((m, dm), F32),
        compiler_params=_cparams(("arbitrary",), vmem),
        name="ple_final",
    )(x, p, gn, wgate, wproj, fn)


def _pad_cols(a, n):
    return jnp.pad(a, ((0, 0), (0, n - a.shape[1])))


def _layer(x, p_emb, lw_, final_norm, *, nb, t, state, shift_prev, caches):
    m, dm = x.shape
    d = lw_["d"]
    dq = lw_["dq"]
    dkv = N_KV * HEAD
    sample = caches is not None
    o_q, o_k, o_v, o_l = 3 * d, 3 * d + dq, 3 * d + dq + dkv, 3 * d + dq + 2 * dkv
    tm = min(m, 1024)
    z = _norm_matmul(x, lw_["norm_mix"], lw_["w_in"], tm, 1280)
    lora_blk = o_l // 512
    nh = d // HEAD
    if not sample:
        feats = _rwkv_prep_prompt(z, lw_["prep"], nb, t, d, 256, lora_blk)
        out_r, wkv = _rwkv_scan(*feats, lw_["r_k"], lw_["ln_w"], lw_["ln_b"], nb, t, d)
        y_a = _attn_prompt(z, lw_["sinks"], nb, t, dq, o_q // dq, o_k // dkv, o_v // dkv)
        z3 = z.reshape(nb, t, -1)
        shift_rows = z3[:, -1]
        k_new = z3[:, -WINDOW:, o_k:o_k + dkv].reshape(nb, WINDOW, N_KV, HEAD)
        v_new = z3[:, -WINDOW:, o_v:o_v + dkv].reshape(nb, WINDOW, N_KV, HEAD)
    else:
        prev_r = shift_prev[:, :3 * d]
        prev_l = jnp.concatenate(
            [_pad_cols(shift_prev[:, 3 * d:3 * d + LORA_W], LORA_PAD),
             _pad_cols(shift_prev[:, 3 * d + LORA_W:3 * d + LORA_W + LORA_A], LORA_PAD),
             shift_prev[:, 3 * d + LORA_W + LORA_A:]], axis=1)
        feats = _rwkv_prep_sample(z, prev_r, prev_l, lw_["prep"], d, lora_blk)
        feats3 = [f.reshape(m, nh, HEAD) for f in feats]
        wkv, out_r3 = _rwkv_step(state, *feats3, lw_["r_k"].reshape(nh, HEAD),
                                 lw_["ln_w"].reshape(nh, HEAD), lw_["ln_b"].reshape(nh, HEAD), 8)
        out_r = out_r3.reshape(m, d)
        ck, cv = caches
        q3 = z[:, o_q:o_q + dq].reshape(m, dq // HEAD, HEAD)
        kn = z[:, o_k:o_k + dkv].reshape(m, 1, N_KV, HEAD)
        vn = z[:, o_v:o_v + dkv].reshape(m, 1, N_KV, HEAD)
        y3, k_new, v_new = _attn_sample(q3, kn, vn, ck, cv, lw_["sinks"].reshape(-1, 1),
                                        lw_["slopes"], 8)
        y_a = y3.reshape(m, dq)
        shift_rows = z
    shift_new = jnp.concatenate(
        [shift_rows[:, :3 * d], shift_rows[:, o_l:o_l + LORA_W],
         shift_rows[:, o_l + LORA_PAD:o_l + LORA_PAD + LORA_A],
         shift_rows[:, o_l + 2 * LORA_PAD:o_l + 2 * LORA_PAD + LORA_G]], axis=1)
    tm2 = min(m, 512)
    x1 = _out_proj(out_r, y_a, x, lw_["w_out"], tm2)
    x2 = _ffn(x1, lw_["norm_ffn"], lw_["w_gate"], lw_["w_up"], lw_["w_down"], tm2, 512)
    y = _ple_final(x2, p_emb, lw_["norm_ple"], lw_["ple_gate"], lw_["ple_proj"], final_norm, tm2)
    return y, wkv, shift_new, k_new, v_new


def kernel(x_prompt, x_sample, state_wkv, state_shift, cache_k, cache_v, p_prompt, p_sample, norm_mix, w_in, mu_shift, rwkv_w0, rwkv_w2, rwkv_a0, rwkv_a2, rwkv_g2, rwkv_k_k, rwkv_k_a, rwkv_r_k, rwkv_ln_w, rwkv_ln_b, attn_sinks, w_out, norm_ffn, w_gate, w_up, w_down, norm_ple, ple_gate, ple_proj, final_norm):
    depth = w_in.shape[0]
    assert depth == 1, "single-layer step"
    bp, t, dm = x_prompt.shape
    bs = x_sample.shape[0]
    assert x_sample.shape[1] == 1
    d = rwkv_w0.shape[1]
    dq = attn_sinks.shape[1] * HEAD
    dkv = N_KV * HEAD
    i = 0
    row = lambda a: a.reshape(1, -1)
    wi = w_in[i]
    rp = 3 * d + LORA_W + LORA_A + LORA_G
    w_in_p = jnp.concatenate(
        [wi[:, :3 * d], wi[:, rp:],
         _pad_cols(wi[:, 3 * d:3 * d + LORA_W], LORA_PAD),
         _pad_cols(wi[:, 3 * d + LORA_W:3 * d + LORA_W + LORA_A], LORA_PAD),
         wi[:, 3 * d + LORA_W + LORA_A:rp]], axis=1).astype(BF16)
    mu = row(mu_shift[i])
    mu_r = mu[:, :3 * d]
    mu_l = jnp.concatenate(
        [_pad_cols(mu[:, 3 * d:3 * d + LORA_W], LORA_PAD),
         _pad_cols(mu[:, 3 * d + LORA_W:3 * d + LORA_W + LORA_A], LORA_PAD),
         mu[:, 3 * d + LORA_W + LORA_A:]], axis=1)
    pad_rows = lambda a: jnp.pad(a, ((0, LORA_PAD - a.shape[0]), (0, 0))).astype(BF16)
    nq = dq // HEAD
    lw_ = {
        "d": d, "dq": dq,
        "norm_mix": row(norm_mix[i]), "w_in": w_in_p,
        "prep": (mu_r, mu_l, row(rwkv_w0[i]), pad_rows(rwkv_w2[i]), row(rwkv_a0[i]),
                 pad_rows(rwkv_a2[i]), rwkv_g2[i].astype(BF16), row(rwkv_k_k[i]), row(rwkv_k_a[i])),
        "r_k": row(rwkv_r_k[i]), "ln_w": row(rwkv_ln_w[i]), "ln_b": row(rwkv_ln_b[i]),
        "sinks": attn_sinks[i],
        "slopes": (2.0 ** (-8.0 * jnp.arange(1, nq + 1, dtype=F32) / nq)).reshape(nq, 1),
        "w_out": w_out[i].astype(BF16), "norm_ffn": row(norm_ffn[i]),
        "w_gate": w_gate[i].astype(BF16), "w_up": w_up[i].astype(BF16),
        "w_down": w_down[i].astype(BF16), "norm_ple": row(norm_ple[i]),
        "ple_gate": ple_gate[i].astype(BF16), "ple_proj": ple_proj[i].astype(BF16),
    }
    fn = row(final_norm)
    yp, wkv_p, shift_p, k_p, v_p = _layer(
        x_prompt.reshape(bp * t, dm), p_prompt[i].reshape(bp * t, -1), lw_, fn,
        nb=bp, t=t, state=None, shift_prev=None, caches=None)
    ys, wkv_s, shift_s, k_s, v_s = _layer(
        x_sample.reshape(bs, dm), p_sample[i].reshape(bs, -1), lw_, fn,
        nb=bs, t=1, state=state_wkv[i], shift_prev=state_shift[i],
        caches=(cache_k[i], cache_v[i]))
    return (yp.reshape(bp, t, dm), ys.reshape(bs, 1, dm),
            wkv_p[None], shift_p[None], k_p[None], v_p[None],
            wkv_s[None], shift_s[None], k_s[None], v_s[None])
```

```python
import functools
import math

import jax
import jax.numpy as jnp
from jax import lax
from jax.experimental import pallas as pl
from jax.experimental.pallas import tpu as pltpu

F32 = jnp.float32
BF16 = jnp.bfloat16

HEAD = 64
LANES = 128
N_KV = 4
WINDOW = 128
LORA_W = 96
LORA_A = 96
LORA_G = 256
LORA_PAD = 128
EPS = 1e-6
GN_EPS = 64e-5
SCALE = HEAD ** -0.5
CHUNK = 64
SCAN_CHUNKS_PER_STEP = 2
DECAY_SCALE = math.exp(-0.5)
VMEM_CAP = 60 * 1024 * 1024


def _cparams(sem, vmem_bytes):
    return pltpu.CompilerParams(dimension_semantics=sem,
                                vmem_limit_bytes=int(min(VMEM_CAP, vmem_bytes)))


def _dot(a, b):
    return jnp.dot(a.astype(BF16), b.astype(BF16), preferred_element_type=F32)


def _dot_nt(a, b):
    return lax.dot_general(a.astype(BF16), b.astype(BF16), (((1,), (1,)), ((), ())),
                           preferred_element_type=F32)


def _dot_tn(a, b):
    return lax.dot_general(a.astype(BF16), b.astype(BF16), (((0,), (0,)), ((), ())),
                           preferred_element_type=F32)


def _split2(x):
    hi = x.astype(BF16)
    lo = (x - hi.astype(F32)).astype(BF16)
    return hi, lo


def _split3(x):
    hi = x.astype(BF16)
    r1 = x - hi.astype(F32)
    mid = r1.astype(BF16)
    lo = (r1 - mid.astype(F32)).astype(BF16)
    return hi, mid, lo


def _head_ones():
    i = lax.broadcasted_iota(jnp.int32, (LANES, LANES), 0) >> 6
    j = lax.broadcasted_iota(jnp.int32, (LANES, LANES), 1) >> 6
    return jnp.where(i == j, 1.0, 0.0).astype(BF16)


def _segsum(x, ones_bd):
    hi, lo = _split2(x)
    return (jnp.dot(hi, ones_bd, preferred_element_type=F32)
            + jnp.dot(lo, ones_bd, preferred_element_type=F32))


def _rms(x, g):
    ms = jnp.mean(x * x, axis=-1, keepdims=True)
    return x * lax.rsqrt(ms + EPS) * g


def _sigmoid(x):
    return 1.0 / (1.0 + jnp.exp(-x))


def _norm_mm_kernel(x_ref, g_ref, w_ref, o_ref, h_ref):
    @pl.when(pl.program_id(1) == 0)
    def _():
        h_ref[...] = _rms(x_ref[...], g_ref[...]).astype(BF16)

    o_ref[...] = jnp.dot(h_ref[...], w_ref[...], preferred_element_type=F32)


def _norm_matmul(x, gain, w, tm, tn):
    m, k = x.shape
    n = w.shape[1]
    vmem = 2 * tm * k * 4 + 2 * k * tn * 2 + 2 * tm * tn * 4 + tm * k * 2 + (4 << 20)
    return pl.pallas_call(
        _norm_mm_kernel,
        grid=(m // tm, n // tn),
        in_specs=[pl.BlockSpec((tm, k), lambda i, j: (i, 0)),
                  pl.BlockSpec((1, k), lambda i, j: (0, 0)),
                  pl.BlockSpec((k, tn), lambda i, j: (0, j))],
        out_specs=pl.BlockSpec((tm, tn), lambda i, j: (i, j)),
        out_shape=jax.ShapeDtypeStruct((m, n), F32),
        scratch_shapes=[pltpu.VMEM((tm, k), BF16)],
        compiler_params=_cparams(("arbitrary", "arbitrary"), vmem),
        name="norm_in_proj",
    )(x, gain, w)


def _prep_core(zr, zpr, zl, zpl, mur_ref, mul_ref, w0_ref, w2_ref, a0_ref, a2_ref, g2_ref,
               kk_ref, ka_ref, out_refs):
    r_o, kh_o, v_o, kk_o, ba_o, lw_o, g_o = out_refs
    d = r_o.shape[-1]
    xr = zr + (zpr - zr) * mur_ref[...]
    xl = zl + (zpl - zl) * mul_ref[...]
    r = xr[:, 0:d]
    k = xr[:, d:2 * d]
    v = xr[:, 2 * d:3 * d]
    xw = xl[:, 0:LORA_PAD]
    xa = xl[:, LORA_PAD:2 * LORA_PAD]
    xg = xl[:, 2 * LORA_PAD:2 * LORA_PAD + LORA_G]
    u = w0_ref[...] + jnp.dot(jnp.tanh(xw).astype(BF16), w2_ref[...], preferred_element_type=F32)
    lw = -DECAY_SCALE * _sigmoid(u)
    a = _sigmoid(a0_ref[...] + jnp.dot(xa.astype(BF16), a2_ref[...], preferred_element_type=F32))
    g = jnp.dot(_sigmoid(xg).astype(BF16), g2_ref[...], preferred_element_type=F32)
    kk = k * kk_ref[...]
    ones_bd = _head_ones()
    kk2 = kk * kk
    ss = jnp.concatenate([_segsum(kk2[:, t * LANES:(t + 1) * LANES], ones_bd)
                          for t in range(d // LANES)], axis=1)
    kk = kk * lax.rsqrt(jnp.maximum(ss, 1e-24))
    kh = k * (1.0 + (a - 1.0) * ka_ref[...])
    r_o[...] = r.astype(r_o.dtype)
    kh_o[...] = kh.astype(kh_o.dtype)
    v_o[...] = v.astype(v_o.dtype)
    kk_o[...] = kk.astype(kk_o.dtype)
    ba_o[...] = (kk * a).astype(ba_o.dtype)
    lw_o[...] = lw
    g_o[...] = g.astype(g_o.dtype)


def _prep_prompt_kernel(zr_ref, zl_ref, mur, mul, w0, w2, a0, a2, g2, kkp, kap,
                        r_o, kh_o, v_o, kk_o, ba_o, lw_o, g_o, cr_ref, cl_ref):
    @pl.when(pl.program_id(1) == 0)
    def _():
        cr_ref[...] = jnp.zeros_like(cr_ref)
        cl_ref[...] = jnp.zeros_like(cl_ref)

    zr = zr_ref[...]
    zl = zl_ref[...]
    tt = zr.shape[0]
    first = lax.broadcasted_iota(jnp.int32, (tt, 1), 0) == 0
    zpr = jnp.where(first, cr_ref[0:1, :], pltpu.roll(zr, 1, 0))
    zpl = jnp.where(first, cl_ref[0:1, :], pltpu.roll(zl, 1, 0))
    cr_ref[0:1, :] = zr[tt - 1:tt, :]
    cl_ref[0:1, :] = zl[tt - 1:tt, :]
    _prep_core(zr, zpr, zl, zpl, mur, mul, w0, w2, a0, a2, g2, kkp, kap,
               (r_o, kh_o, v_o, kk_o, ba_o, lw_o, g_o))


def _prep_sample_kernel(zr_ref, zl_ref, pr_ref, pl_ref, mur, mul, w0, w2, a0, a2, g2, kkp, kap,
                        r_o, kh_o, v_o, kk_o, ba_o, lw_o, g_o):
    _prep_core(zr_ref[...], pr_ref[...], zl_ref[...], pl_ref[...], mur, mul, w0, w2, a0, a2, g2,
               kkp, kap, (r_o, kh_o, v_o, kk_o, ba_o, lw_o, g_o))


def _const_spec(shape):
    return pl.BlockSpec(shape, lambda *_: (0,) * len(shape))


def _prep_param_specs(d):
    return [_const_spec((1, 3 * d)), _const_spec((1, 512)), _const_spec((1, d)),
            _const_spec((LORA_PAD, d)), _const_spec((1, d)), _const_spec((LORA_PAD, d)),
            _const_spec((LORA_G, d)), _const_spec((1, d)), _const_spec((1, d))]


def _rwkv_prep_prompt(z, params, nb, t, d, tt, lora_blk):
    m = nb * t
    nt = t // tt
    row = lambda b, i: (b * nt + i, 0)
    act = jax.ShapeDtypeStruct((m, d), BF16)
    out_shape = [act] * 5 + [jax.ShapeDtypeStruct((m, d), F32), act]
    vmem = 2 * tt * (3 * d + 512) * 4 + 14 * tt * d * 4 + 24 * tt * d * 4 + (8 << 20)
    return pl.pallas_call(
        _prep_prompt_kernel,
        grid=(nb, nt),
        in_specs=[pl.BlockSpec((tt, 3 * d), row),
                  pl.BlockSpec((tt, 512), lambda b, i: (b * nt + i, lora_blk))]
                 + _prep_param_specs(d),
        out_specs=[pl.BlockSpec((tt, d), row)] * 7,
        out_shape=out_shape,
        scratch_shapes=[pltpu.VMEM((8, 3 * d), F32), pltpu.VMEM((8, 512), F32)],
        compiler_params=_cparams(("arbitrary", "arbitrary"), vmem),
        name="rwkv_prep_prompt",
    )(z, z, *params)


def _rwkv_prep_sample(z, prev_r, prev_l, params, d, lora_blk):
    m = z.shape[0]
    out = jax.ShapeDtypeStruct((m, d), F32)
    vmem = 4 * m * (3 * d + 512) * 4 + 14 * m * d * 4 + 24 * m * d * 4 + (8 << 20)
    return pl.pallas_call(
        _prep_sample_kernel,
        grid=(1,),
        in_specs=[pl.BlockSpec((m, 3 * d), lambda i: (0, 0)),
                  pl.BlockSpec((m, 512), lambda i: (0, lora_blk)),
                  _const_spec((m, 3 * d)), _const_spec((m, 512))]
                 + _prep_param_specs(d),
        out_specs=[pl.BlockSpec((m, d), lambda i: (0, 0))] * 7,
        out_shape=[out] * 7,
        compiler_params=_cparams(("arbitrary",), vmem),
        name="rwkv_prep_sample",
    )(z, z, prev_r, prev_l, *params)


def _rwkv_post(y, r, kh, v, g, rk, lnw, lnb, ones_bd, n_tiles):
    rows = y.shape[0] // n_tiles
    par = lambda a: jnp.concatenate(
        [jnp.broadcast_to(a[:, p * LANES:(p + 1) * LANES], (rows, LANES)) for p in range(n_tiles)], axis=0)
    inv = 1.0 / HEAD
    mean = _segsum(y, ones_bd) * inv
    dlt = y - mean
    var = _segsum(dlt * dlt, ones_bd) * inv
    yn = dlt * lax.rsqrt(var + GN_EPS) * par(lnw) + par(lnb)
    bonus = _segsum(r * kh * par(rk), ones_bd) * v
    return (yn + bonus) * g


def _pair_diag(x, lane_lo):
    return jnp.concatenate([jnp.where(lane_lo, x, 0.0), jnp.where(lane_lo, 0.0, x)], axis=0)


def _scan_kernel(r_ref, kh_ref, v_ref, kk_ref, ba_ref, lw_ref, g_ref, rk_ref, lnw_ref, lnb_ref,
                 o_ref, st_ref, s_scr):
    c = pl.program_id(1)
    n_pairs = s_scr.shape[0]
    C = CHUNK

    @pl.when(c == 0)
    def _():
        s_scr[...] = jnp.zeros_like(s_scr)

    row = lax.broadcasted_iota(jnp.int32, (C, LANES), 0)
    lane = lax.broadcasted_iota(jnp.int32, (C, LANES), 1)
    src = lane & (HEAD - 1)
    strict = src < row
    incl = src <= row
    eye_w = jnp.where(src == row, 1.0, 0.0)
    lane_lo = lane < HEAD
    sq_r = lax.broadcasted_iota(jnp.int32, (LANES, LANES), 0) >> 6
    sq_c = lax.broadcasted_iota(jnp.int32, (LANES, LANES), 1) >> 6
    same_head = sq_r == sq_c
    ones_bd = jnp.where(same_head, 1.0, 0.0).astype(BF16)
    tri = jnp.where(lax.broadcasted_iota(jnp.int32, (C, C), 1)
                    <= lax.broadcasted_iota(jnp.int32, (C, C), 0), 1.0, 0.0).astype(BF16)

    pairs = range(n_pairs)
    tile = lambda x, p: x[:, p * LANES:(p + 1) * LANES]
    rows_of = lambda x: jnp.concatenate([tile(x, p) for p in pairs], axis=0)
    lanes_of = lambda x: jnp.concatenate([x[p * C:(p + 1) * C] for p in pairs], axis=1)
    pd = lambda x: _pair_diag(x, lane_lo)

    def chunk(sub):
        rs = slice(sub * C, (sub + 1) * C)
        f32 = lambda ref: ref[rs, :].astype(F32)
        r, kh, v = f32(r_ref), f32(kh_ref), f32(v_ref)
        kk, ba, lw = f32(kk_ref), f32(ba_ref), lw_ref[rs, :]
        l_hi, l_mid, l_lo = _split3(lw)
        cs = (jnp.dot(tri, l_hi, preferred_element_type=F32)
              + jnp.dot(tri, l_mid, preferred_element_type=F32)
              + jnp.dot(tri, l_lo, preferred_element_type=F32))
        c_end = cs[C - 1:C, :]
        e_neg = jnp.exp(-cs)
        e_end = jnp.exp(c_end - cs)
        r_t = r * jnp.exp(cs)
        k_x = kk * jnp.exp(cs - lw)
        k_t = kh * e_neg
        b_t = ba * e_neg
        k_e = kh * e_end
        b_e = ba * e_end
        g_end = jnp.exp(c_end)

        gram = [_dot_nt(jnp.concatenate([tile(k_x, p), tile(r_t, p)], axis=0),
                        jnp.concatenate([pd(tile(k_t, p)), pd(tile(b_t, p))], axis=0)) for p in pairs]
        a_kk = [jnp.where(strict, gm[0:C, 0:LANES], 0.0) for gm in gram]
        a_kb = [jnp.where(strict, gm[0:C, LANES:2 * LANES], 0.0) for gm in gram]
        a_rk = [jnp.where(incl, gm[C:2 * C, 0:LANES], 0.0) for gm in gram]
        a_rb = [jnp.where(incl, gm[C:2 * C, LANES:2 * LANES], 0.0) for gm in gram]

        pw = [_dot(a, pd(a)) for a in a_kb]
        t_inv = [eye_w - a for a in a_kb]
        n_lvl = int(math.log2(C)) - 1
        for lvl in range(n_lvl):
            last = lvl == n_lvl - 1
            prod = [_dot(t_inv[p] if last else jnp.concatenate([t_inv[p], pw[p]], axis=0), pd(pw[p]))
                    for p in pairs]
            t_inv = [t_inv[p] + prod[p][0:C] for p in pairs]
            if not last:
                pw = [prod[p][C:2 * C] for p in pairs]

        s_bd = [s_scr[p] for p in pairs]
        v_bd = [pd(tile(v, p)) for p in pairs]
        w_mat = [_dot_nt(tile(k_x, p), s_bd[p]) + _dot(a_kk[p], v_bd[p]) for p in pairs]
        u = [_dot(t_inv[p], pd(w_mat[p])) for p in pairs]
        y = [_dot_nt(tile(r_t, p), s_bd[p]) + _dot(a_rk[p], v_bd[p]) - _dot(a_rb[p], pd(u[p]))
             for p in pairs]
        for p in pairs:
            upd = _dot_tn(jnp.concatenate([tile(v, p), u[p]], axis=0),
                          jnp.concatenate([tile(k_e, p), -tile(b_e, p)], axis=0))
            s_scr[p] = s_bd[p] * tile(g_end, p) + jnp.where(same_head, upd, 0.0)

        y_rows = jnp.concatenate(y, axis=0)
        out = _rwkv_post(y_rows, rows_of(r), rows_of(kh), rows_of(v), rows_of(f32(g_ref)),
                         rk_ref[...], lnw_ref[...], lnb_ref[...], ones_bd, n_pairs)
        o_ref[rs, :] = lanes_of(out)

    for sub in range(r_ref.shape[0] // C):
        chunk(sub)

    @pl.when(c == pl.num_programs(1) - 1)
    def _():
        for p in range(n_pairs):
            s = s_scr[p]
            st_ref[0, 2 * p] = s[0:HEAD, 0:HEAD]
            st_ref[0, 2 * p + 1] = s[HEAD:2 * HEAD, HEAD:2 * HEAD]


def _rwkv_scan(r, kh, v, kk, ba, lw, g, rk, lnw, lnb, nb, t, d):
    rows = CHUNK * SCAN_CHUNKS_PER_STEP
    nc = t // rows
    nh = d // HEAD
    blk = pl.BlockSpec((rows, d), lambda b, c: (b * nc + c, 0))
    par = _const_spec((1, d))
    vmem = 16 * rows * d * 4 + 3 * (d // LANES) * LANES * LANES * 4 + (16 << 20)
    return pl.pallas_call(
        _scan_kernel,
        grid=(nb, nc),
        in_specs=[blk] * 7 + [par] * 3,
        out_specs=[blk, pl.BlockSpec((1, nh, HEAD, HEAD), lambda b, c: (b, 0, 0, 0))],
        out_shape=[jax.ShapeDtypeStruct((nb * t, d), F32),
                   jax.ShapeDtypeStruct((nb, nh, HEAD, HEAD), F32)],
        scratch_shapes=[pltpu.VMEM((d // LANES, LANES, LANES), F32)],
        compiler_params=_cparams(("arbitrary", "arbitrary"), vmem),
        name="rwkv_scan",
    )(r, kh, v, kk, ba, lw, g, rk, lnw, lnb)


def _step_kernel(s_ref, r_ref, kh_ref, v_ref, kk_ref, ba_ref, lw_ref, g_ref, rk_ref, lnw_ref,
                 lnb_ref, so_ref, o_ref):
    bb, nh = r_ref.shape[0], r_ref.shape[1]
    eye = (lax.broadcasted_iota(jnp.int32, (HEAD, HEAD), 0)
           == lax.broadcasted_iota(jnp.int32, (HEAD, HEAD), 1))
    inv = 1.0 / HEAD

    def body(b, carry):
        r_b, kh_b, v_b = r_ref[b], kh_ref[b], v_ref[b]
        kk_b, ba_b, dec_b = kk_ref[b], ba_ref[b], jnp.exp(lw_ref[b])
        rows = []
        for h in range(nh):
            s = s_ref[b, h]
            sa = -jnp.sum(s * kk_b[h:h + 1], axis=-1, keepdims=True)
            v_col = jnp.sum(jnp.where(eye, v_b[h:h + 1], 0.0), axis=-1, keepdims=True)
            s_new = s * dec_b[h:h + 1] + sa * ba_b[h:h + 1] + v_col * kh_b[h:h + 1]
            so_ref[b, h] = s_new
            y_col = jnp.sum(s_new * r_b[h:h + 1], axis=-1, keepdims=True)
            rows.append(jnp.sum(jnp.where(eye, y_col, 0.0), axis=0, keepdims=True))
        y = jnp.concatenate(rows, axis=0)
        mean = jnp.sum(y, axis=-1, keepdims=True) * inv
        dlt = y - mean
        var = jnp.sum(dlt * dlt, axis=-1, keepdims=True) * inv
        yn = dlt * lax.rsqrt(var + GN_EPS) * lnw_ref[...] + lnb_ref[...]
        bonus = jnp.sum(r_b * kh_b * rk_ref[...], axis=-1, keepdims=True) * v_b
        o_ref[b] = (yn + bonus) * g_ref[b]
        return carry

    lax.fori_loop(0, bb, body, 0)


def _rwkv_step(state, r, kh, v, kk, ba, lw, g, rk, lnw, lnb, bb):
    nb, nh = state.shape[0], state.shape[1]
    vec = pl.BlockSpec((bb, nh, HEAD), lambda i: (i, 0, 0))
    st = pl.BlockSpec((bb, nh, HEAD, HEAD), lambda i: (i, 0, 0, 0))
    par = _const_spec((nh, HEAD))
    vmem = 4 * bb * nh * HEAD * LANES * 4 + (16 << 20)
    return pl.pallas_call(
        _step_kernel,
        grid=(nb // bb,),
        in_specs=[st] + [vec] * 7 + [par] * 3,
        out_specs=[st, vec],
        out_shape=[jax.ShapeDtypeStruct(state.shape, F32),
                   jax.ShapeDtypeStruct((nb, nh, HEAD), F32)],
        compiler_params=_cparams(("arbitrary",), vmem),
        name="rwkv_step",
    )(state, r, kh, v, kk, ba, lw, g, rk, lnw, lnb)


def _alibi_slope(h, n_heads):
    return 2.0 ** (-8.0 * (h + 1) / n_heads)


def _attn_prompt_kernel(sink_ref, q_ref, kp_ref, kc_ref, vp_ref, vc_ref, o_ref):
    n = pl.program_id(1)
    nq = q_ref.shape[1] // HEAD
    group = nq // N_KV
    q = q_ref[...]
    k2 = jnp.concatenate([kp_ref[...], kc_ref[...]], axis=0).astype(BF16)
    v2 = jnp.concatenate([vp_ref[...], vc_ref[...]], axis=0).astype(BF16)
    qi = lax.broadcasted_iota(jnp.int32, (WINDOW, 2 * WINDOW), 0)
    si = lax.broadcasted_iota(jnp.int32, (WINDOW, 2 * WINDOW), 1)
    dist = qi + WINDOW - si
    first_key = jnp.where(n > 0, 0, WINDOW)
    valid = (dist >= 0) & (dist < WINDOW) & (si >= first_key)
    distf = dist.astype(F32)
    outs = []
    for h in range(nq):
        j = h // group
        qh = q[:, h * HEAD:(h + 1) * HEAD]
        s = _dot_nt(qh, k2[:, j * HEAD:(j + 1) * HEAD]) * SCALE - _alibi_slope(h, nq) * distf
        s = jnp.where(valid, s, -jnp.inf)
        sink = sink_ref[h]
        mx = jnp.maximum(jnp.max(s, axis=-1, keepdims=True), sink)
        p = jnp.exp(s - mx)
        den = jnp.sum(p, axis=-1, keepdims=True) + jnp.exp(sink - mx)
        outs.append(_dot(p, v2[:, j * HEAD:(j + 1) * HEAD]) / den)
    o_ref[...] = jnp.concatenate(outs, axis=1)


def _attn_prompt(z, sinks, nb, t, dq, q_blk, k_blk, v_blk):
    nblk = t // WINDOW
    dkv = N_KV * HEAD
    cur = lambda col: (lambda b, n: (b * nblk + n, col))
    prev = lambda col: (lambda b, n: (b * nblk + jnp.maximum(n - 1, 0), col))
    vmem = 4 * WINDOW * (dq * 2 + 4 * dkv) * 4 + (24 << 20)
    return pl.pallas_call(
        _attn_prompt_kernel,
        grid=(nb, nblk),
        in_specs=[pl.BlockSpec(memory_space=pltpu.SMEM),
                  pl.BlockSpec((WINDOW, dq), cur(q_blk)),
                  pl.BlockSpec((WINDOW, dkv), prev(k_blk)),
                  pl.BlockSpec((WINDOW, dkv), cur(k_blk)),
                  pl.BlockSpec((WINDOW, dkv), prev(v_blk)),
                  pl.BlockSpec((WINDOW, dkv), cur(v_blk))],
        out_specs=pl.BlockSpec((WINDOW, dq), lambda b, n: (b * nblk + n, 0)),
        out_shape=jax.ShapeDtypeStruct((nb * t, dq), F32),
        compiler_params=_cparams(("arbitrary", "arbitrary"), vmem),
        name="attn_prompt",
    )(sinks, z, z, z, z, z)


def _attn_sample_kernel(q_ref, kn_ref, vn_ref, ck_ref, cv_ref, sink_ref, slope_ref,
                        o_ref, ko_ref, vo_ref):
    bb, nq = q_ref.shape[0], q_ref.shape[1]
    group = nq // N_KV
    dkv = N_KV * HEAD
    hrow = lax.broadcasted_iota(jnp.int32, (nq, dkv), 0) // group
    hlane = lax.broadcasted_iota(jnp.int32, (nq, dkv), 1) >> 6
    own = hrow == hlane
    si = lax.broadcasted_iota(jnp.int32, (nq, WINDOW), 1)
    distf = (WINDOW - si).astype(F32)
    valid = si >= 1
    sink = sink_ref[...]
    slope = slope_ref[...]

    def body(b, carry):
        q = q_ref[b]
        q_bd = jnp.where(own, jnp.concatenate([q] * N_KV, axis=1), 0.0)
        ck = ck_ref[b]
        cv = cv_ref[b]
        kn = kn_ref[b]
        vn = vn_ref[b]
        s_c = _dot_nt(q_bd, ck) * SCALE - slope * distf
        s_c = jnp.where(valid, s_c, -jnp.inf)
        s_n = jnp.sum(q_bd * kn, axis=-1, keepdims=True) * SCALE
        mx = jnp.maximum(jnp.maximum(jnp.max(s_c, axis=-1, keepdims=True), s_n), sink)
        p_c = jnp.exp(s_c - mx)
        p_n = jnp.exp(s_n - mx)
        den = jnp.sum(p_c, axis=-1, keepdims=True) + p_n + jnp.exp(sink - mx)
        o_all = (_dot(p_c, cv) + p_n * vn) / den
        o_ref[b] = jnp.concatenate(
            [o_all[j * group:(j + 1) * group, j * HEAD:(j + 1) * HEAD] for j in range(N_KV)], axis=0)
        ko_ref[b] = jnp.concatenate([ck[1:], kn], axis=0)
        vo_ref[b] = jnp.concatenate([cv[1:], vn], axis=0)
        return carry

    lax.fori_loop(0, bb, body, 0)


def _attn_sample(q, kn, vn, ck, cv, sinks, slopes, bb):
    nb, nq = q.shape[0], q.shape[1]
    dkv = N_KV * HEAD
    cache = pl.BlockSpec((bb, WINDOW, dkv), lambda i: (i, 0, 0))
    new = pl.BlockSpec((bb, 1, dkv), lambda i: (i, 0, 0))
    qs = pl.BlockSpec((bb, nq, HEAD), lambda i: (i, 0, 0))
    vmem = 8 * bb * WINDOW * dkv * 4 + (16 << 20)
    return pl.pallas_call(
        _attn_sample_kernel,
        grid=(nb // bb,),
        in_specs=[qs, new, new, cache, cache, _const_spec((nq, 1)), _const_spec((nq, 1))],
        out_specs=[qs, cache, cache],
        out_shape=[jax.ShapeDtypeStruct((nb, nq, HEAD), F32),
                   jax.ShapeDtypeStruct(ck.shape, F32),
                   jax.ShapeDtypeStruct(cv.shape, F32)],
        compiler_params=_cparams(("arbitrary",), vmem),
        name="attn_sample",
    )(q, kn, vn, ck, cv, sinks, slopes)


def _out_proj_kernel(a_ref, b_ref, x_ref, w_ref, o_ref):
    d = a_ref.shape[1]
    o_ref[...] = (x_ref[...]
                  + jnp.dot(a_ref[...].astype(BF16), w_ref[0:d, :], preferred_element_type=F32)
                  + jnp.dot(b_ref[...].astype(BF16), w_ref[d:, :], preferred_element_type=F32))


def _out_proj(a, b, x, w, tm):
    m, d = a.shape
    dm = x.shape[1]
    vmem = 4 * tm * d * 4 + 4 * tm * dm * 4 + 2 * 2 * d * dm * 2 + (8 << 20)
    return pl.pallas_call(
        _out_proj_kernel,
        grid=(m // tm,),
        in_specs=[pl.BlockSpec((tm, d), lambda i: (i, 0)),
                  pl.BlockSpec((tm, d), lambda i: (i, 0)),
                  pl.BlockSpec((tm, dm), lambda i: (i, 0)),
                  _const_spec((2 * d, dm))],
        out_specs=pl.BlockSpec((tm, dm), lambda i: (i, 0)),
        out_shape=jax.ShapeDtypeStruct((m, dm), F32),
        compiler_params=_cparams(("arbitrary",), vmem),
        name="out_proj",
    )(a, b, x, w)


def _ffn_kernel(x_ref, g_ref, wg_ref, wu_ref, wd_ref, o_ref, h_ref, acc_ref):
    j = pl.program_id(1)

    @pl.when(j == 0)
    def _():
        h_ref[...] = _rms(x_ref[...], g_ref[...]).astype(BF16)
        acc_ref[...] = jnp.zeros_like(acc_ref)

    h = h_ref[...]
    gate = jnp.dot(h, wg_ref[...], preferred_element_type=F32)
    up = jnp.dot(h, wu_ref[...], preferred_element_type=F32)
    act = gate * _sigmoid(gate) * up
    acc_ref[...] += jnp.dot(act.astype(BF16), wd_ref[...], preferred_element_type=F32)

    @pl.when(j == pl.num_programs(1) - 1)
    def _():
        o_ref[...] = x_ref[...] + acc_ref[...]


def _ffn(x, gain, wg, wu, wd, tm, tf):
    m, dm = x.shape
    f = wg.shape[1]
    vmem = 4 * tm * dm * 4 + tm * dm * 2 + tm * dm * 4 + 6 * dm * tf * 2 + 4 * tm * tf * 4 + (6 << 20)
    return pl.pallas_call(
        _ffn_kernel,
        grid=(m // tm, f // tf),
        in_specs=[pl.BlockSpec((tm, dm), lambda i, j: (i, 0)),
                  _const_spec((1, dm)),
                  pl.BlockSpec((dm, tf), lambda i, j: (0, j)),
                  pl.BlockSpec((dm, tf), lambda i, j: (0, j)),
                  pl.BlockSpec((tf, dm), lambda i, j: (j, 0))],
        out_specs=pl.BlockSpec((tm, dm), lambda i, j: (i, 0)),
        out_shape=jax.ShapeDtypeStruct((m, dm), F32),
        scratch_shapes=[pltpu.VMEM((tm, dm), BF16), pltpu.VMEM((tm, dm), F32)],
        compiler_params=_cparams(("arbitrary", "arbitrary"), vmem),
        name="ffn",
    )(x, gain, wg, wu, wd)


def _ple_kernel(x_ref, p_ref, gn_ref, wgate_ref, wproj_ref, fn_ref, o_ref):
    x = x_ref[...]
    gate = _sigmoid(jnp.dot(_rms(x, gn_ref[...]).astype(BF16), wgate_ref[...],
                            preferred_element_type=F32))
    emb = jnp.dot(p_ref[...].astype(BF16), wproj_ref[...], preferred_element_type=F32)
    o_ref[...] = _rms(x + gate * emb, fn_ref[...])


def _ple_final(x, p, gn, wgate, wproj, fn, tm):
    m, dm = x.shape
    dp = p.shape[1]
    vmem = 4 * tm * dm * 4 + 2 * tm * dp * 4 + 2 * dm * dm * 2 + 2 * dp * dm * 2 + 4 * tm * dm * 4 + (6 << 20)
    return pl.pallas_call(
        _ple_kernel,
        grid=(m // tm,),
        in_specs=[pl.BlockSpec((tm, dm), lambda i: (i, 0)),
                  pl.BlockSpec((tm, dp), lambda i: (i, 0)),
                  _const_spec((1, dm)), _const_spec((dm, dm)), _const_spec((dp, dm)),
                  _const_spec((1, dm))],
        out_specs=pl.BlockSpec((tm, dm), lambda i: (i, 0)),
        out_shape=jax.ShapeDtypeStruct((m, dm), F32),
        compiler_params=_cparams(("arbitrary",), vmem),
        name="ple_final",
    )(x, p, gn, wgate, wproj, fn)


def _pad_cols(a, n):
    return jnp.pad(a, ((0, 0), (0, n - a.shape[1])))


def _layer(x, p_emb, lw_, final_norm, *, nb, t, state, shift_prev, caches):
    m, dm = x.shape
    d = lw_["d"]
    dq = lw_["dq"]
    dkv = N_KV * HEAD
    sample = caches is not None
    o_q, o_k, o_v, o_l = 3 * d, 3 * d + dq, 3 * d + dq + dkv, 3 * d + dq + 2 * dkv
    tm = min(m, 1024)
    z = _norm_matmul(x, lw_["norm_mix"], lw_["w_in"], tm, 1280)
    lora_blk = o_l // 512
    nh = d // HEAD
    if not sample:
        feats = _rwkv_prep_prompt(z, lw_["prep"], nb, t, d, 256, lora_blk)
        out_r, wkv = _rwkv_scan(*feats, lw_["r_k"], lw_["ln_w"], lw_["ln_b"], nb, t, d)
        y_a = _attn_prompt(z, lw_["sinks"], nb, t, dq, o_q // dq, o_k // dkv, o_v // dkv)
        z3 = z.reshape(nb, t, -1)
        shift_rows = z3[:, -1]
        k_new = z3[:, -WINDOW:, o_k:o_k + dkv].reshape(nb, WINDOW, N_KV, HEAD)
        v_new = z3[:, -WINDOW:, o_v:o_v + dkv].reshape(nb, WINDOW, N_KV, HEAD)
    else:
        prev_r = shift_prev[:, :3 * d]
        prev_l = jnp.concatenate(
            [_pad_cols(shift_prev[:, 3 * d:3 * d + LORA_W], LORA_PAD),
             _pad_cols(shift_prev[:, 3 * d + LORA_W:3 * d + LORA_W + LORA_A], LORA_PAD),
             shift_prev[:, 3 * d + LORA_W + LORA_A:]], axis=1)
        feats = _rwkv_prep_sample(z, prev_r, prev_l, lw_["prep"], d, lora_blk)
        feats3 = [f.reshape(m, nh, HEAD) for f in feats]
        wkv, out_r3 = _rwkv_step(state, *feats3, lw_["r_k"].reshape(nh, HEAD),
                                 lw_["ln_w"].reshape(nh, HEAD), lw_["ln_b"].reshape(nh, HEAD), 8)
        out_r = out_r3.reshape(m, d)
        ck, cv = caches
        q3 = z[:, o_q:o_q + dq].reshape(m, dq // HEAD, HEAD)
        kn = z[:, o_k:o_k + dkv].reshape(m, 1, dkv)
        vn = z[:, o_v:o_v + dkv].reshape(m, 1, dkv)
        y3, k_new, v_new = _attn_sample(q3, kn, vn, ck.reshape(m, WINDOW, dkv),
                                        cv.reshape(m, WINDOW, dkv), lw_["sinks"].reshape(-1, 1),
                                        lw_["slopes"], 8)
        y_a = y3.reshape(m, dq)
        k_new = k_new.reshape(m, WINDOW, N_KV, HEAD)
        v_new = v_new.reshape(m, WINDOW, N_KV, HEAD)
        shift_rows = z
    shift_new = jnp.concatenate(
        [shift_rows[:, :3 * d], shift_rows[:, o_l:o_l + LORA_W],
         shift_rows[:, o_l + LORA_PAD:o_l + LORA_PAD + LORA_A],
         shift_rows[:, o_l + 2 * LORA_PAD:o_l + 2 * LORA_PAD + LORA_G]], axis=1)
    tm2 = min(m, 512)
    x1 = _out_proj(out_r, y_a, x, lw_["w_out"], tm2)
    x2 = _ffn(x1, lw_["norm_ffn"], lw_["w_gate"], lw_["w_up"], lw_["w_down"], tm2, 512)
    y = _ple_final(x2, p_emb, lw_["norm_ple"], lw_["ple_gate"], lw_["ple_proj"], final_norm, tm2)
    return y, wkv, shift_new, k_new, v_new


def kernel(x_prompt, x_sample, state_wkv, state_shift, cache_k, cache_v, p_prompt, p_sample, norm_mix, w_in, mu_shift, rwkv_w0, rwkv_w2, rwkv_a0, rwkv_a2, rwkv_g2, rwkv_k_k, rwkv_k_a, rwkv_r_k, rwkv_ln_w, rwkv_ln_b, attn_sinks, w_out, norm_ffn, w_gate, w_up, w_down, norm_ple, ple_gate, ple_proj, final_norm):
    depth = w_in.shape[0]
    assert depth == 1, "single-layer step"
    bp, t, dm = x_prompt.shape
    bs = x_sample.shape[0]
    assert x_sample.shape[1] == 1
    d = rwkv_w0.shape[1]
    dq = attn_sinks.shape[1] * HEAD
    dkv = N_KV * HEAD
    i = 0
    row = lambda a: a.reshape(1, -1)
    wi = w_in[i]
    rp = 3 * d + LORA_W + LORA_A + LORA_G
    w_in_p = jnp.concatenate(
        [wi[:, :3 * d], wi[:, rp:],
         _pad_cols(wi[:, 3 * d:3 * d + LORA_W], LORA_PAD),
         _pad_cols(wi[:, 3 * d + LORA_W:3 * d + LORA_W + LORA_A], LORA_PAD),
         wi[:, 3 * d + LORA_W + LORA_A:rp]], axis=1).astype(BF16)
    mu = row(mu_shift[i])
    mu_r = mu[:, :3 * d]
    mu_l = jnp.concatenate(
        [_pad_cols(mu[:, 3 * d:3 * d + LORA_W], LORA_PAD),
         _pad_cols(mu[:, 3 * d + LORA_W:3 * d + LORA_W + LORA_A], LORA_PAD),
         mu[:, 3 * d + LORA_W + LORA_A:]], axis=1)
    pad_rows = lambda a: jnp.pad(a, ((0, LORA_PAD - a.shape[0]), (0, 0))).astype(BF16)
    nq = dq // HEAD
    lw_ = {
        "d": d, "dq": dq,
        "norm_mix": row(norm_mix[i]), "w_in": w_in_p,
        "prep": (mu_r, mu_l, row(rwkv_w0[i]), pad_rows(rwkv_w2[i]), row(rwkv_a0[i]),
                 pad_rows(rwkv_a2[i]), rwkv_g2[i].astype(BF16), row(rwkv_k_k[i]), row(rwkv_k_a[i])),
        "r_k": row(rwkv_r_k[i]), "ln_w": row(rwkv_ln_w[i]), "ln_b": row(rwkv_ln_b[i]),
        "sinks": attn_sinks[i],
        "slopes": (2.0 ** (-8.0 * jnp.arange(1, nq + 1, dtype=F32) / nq)).reshape(nq, 1),
        "w_out": w_out[i].astype(BF16), "norm_ffn": row(norm_ffn[i]),
        "w_gate": w_gate[i].astype(BF16), "w_up": w_up[i].astype(BF16),
        "w_down": w_down[i].astype(BF16), "norm_ple": row(norm_ple[i]),
        "ple_gate": ple_gate[i].astype(BF16), "ple_proj": ple_proj[i].astype(BF16),
    }
    fn = row(final_norm)
    yp, wkv_p, shift_p, k_p, v_p = _layer(
        x_prompt.reshape(bp * t, dm), p_prompt[i].reshape(bp * t, -1), lw_, fn,
        nb=bp, t=t, state=None, shift_prev=None, caches=None)
    ys, wkv_s, shift_s, k_s, v_s = _layer(
        x_sample.reshape(bs, dm), p_sample[i].reshape(bs, -1), lw_, fn,
        nb=bs, t=1, state=state_wkv[i], shift_prev=state_shift[i],
        caches=(cache_k[i], cache_v[i]))
    return (yp.reshape(bp, t, dm), ys.reshape(bs, 1, dm),
            wkv_p[None], shift_p[None], k_p[None], v_p[None],
            wkv_s[None], shift_s[None], k_s[None], v_s[None])
```

```python
import functools
import math

import jax
import jax.numpy as jnp
from jax import lax
from jax.experimental import pallas as pl
from jax.experimental.pallas import tpu as pltpu

F32 = jnp.float32
BF16 = jnp.bfloat16

HEAD = 64
LANES = 128
N_KV = 4
WINDOW = 128
LORA_W = 96
LORA_A = 96
LORA_G = 256
LORA_PAD = 128
EPS = 1e-6
GN_EPS = 64e-5
SCALE = HEAD ** -0.5
assert math.log2(SCALE).is_integer()
CHUNK = 64
SCAN_CHUNKS_PER_STEP = 2
DECAY_SCALE = math.exp(-0.5)
VMEM_CAP = 60 * 1024 * 1024


def _cparams(sem, vmem_bytes):
    return pltpu.CompilerParams(dimension_semantics=sem,
                                vmem_limit_bytes=int(min(VMEM_CAP, vmem_bytes)))


def _dot(a, b):
    return jnp.dot(a.astype(BF16), b.astype(BF16), preferred_element_type=F32)


def _dot_nt(a, b):
    return lax.dot_general(a.astype(BF16), b.astype(BF16), (((1,), (1,)), ((), ())),
                           preferred_element_type=F32)


def _dot_tn(a, b):
    return lax.dot_general(a.astype(BF16), b.astype(BF16), (((0,), (0,)), ((), ())),
                           preferred_element_type=F32)


def _split2(x):
    hi = x.astype(BF16)
    lo = (x - hi.astype(F32)).astype(BF16)
    return hi, lo


def _split3(x):
    hi = x.astype(BF16)
    r1 = x - hi.astype(F32)
    mid = r1.astype(BF16)
    lo = (r1 - mid.astype(F32)).astype(BF16)
    return hi, mid, lo


def _head_ones():
    i = lax.broadcasted_iota(jnp.int32, (LANES, LANES), 0) >> 6
    j = lax.broadcasted_iota(jnp.int32, (LANES, LANES), 1) >> 6
    return jnp.where(i == j, 1.0, 0.0).astype(BF16)


def _segsum(x, ones_bd):
    hi, lo = _split2(x)
    return (jnp.dot(hi, ones_bd, preferred_element_type=F32)
            + jnp.dot(lo, ones_bd, preferred_element_type=F32))


def _rms(x, g):
    ms = jnp.mean(x * x, axis=-1, keepdims=True)
    return x * lax.rsqrt(ms + EPS) * g


def _sigmoid(x):
    return 1.0 / (1.0 + jnp.exp(-x))


def _norm_mm_kernel(x_ref, g_ref, w_ref, o_ref, h_ref):
    @pl.when(pl.program_id(1) == 0)
    def _():
        h_ref[...] = _rms(x_ref[...], g_ref[...]).astype(BF16)

    o_ref[...] = jnp.dot(h_ref[...], w_ref[...], preferred_element_type=F32).astype(o_ref.dtype)


def _norm_matmul(x, gain, w, tm, tn):
    m, k = x.shape
    n = w.shape[1]
    vmem = 2 * tm * k * 4 + 2 * k * tn * 2 + 2 * tm * tn * 4 + tm * k * 2 + (4 << 20)
    return pl.pallas_call(
        _norm_mm_kernel,
        grid=(m // tm, n // tn),
        in_specs=[pl.BlockSpec((tm, k), lambda i, j: (i, 0)),
                  pl.BlockSpec((1, k), lambda i, j: (0, 0)),
                  pl.BlockSpec((k, tn), lambda i, j: (0, j))],
        out_specs=pl.BlockSpec((tm, tn), lambda i, j: (i, j)),
        out_shape=jax.ShapeDtypeStruct((m, n), BF16),
        scratch_shapes=[pltpu.VMEM((tm, k), BF16)],
        compiler_params=_cparams(("arbitrary", "arbitrary"), vmem),
        name="norm_in_proj",
    )(x, gain, w)


def _prep_core(zr, zpr, zl, zpl, mur_ref, mul_ref, w0_ref, w2_ref, a0_ref, a2_ref, g2_ref,
               kk_ref, ka_ref, out_refs):
    r_o, kh_o, v_o, kk_o, ba_o, lw_o, g_o = out_refs
    d = r_o.shape[-1]
    xr = zr + (zpr - zr) * mur_ref[...]
    xl = zl + (zpl - zl) * mul_ref[...]
    r = xr[:, 0:d]
    k = xr[:, d:2 * d]
    v = xr[:, 2 * d:3 * d]
    xw = xl[:, 0:LORA_PAD]
    xa = xl[:, LORA_PAD:2 * LORA_PAD]
    xg = xl[:, 2 * LORA_PAD:2 * LORA_PAD + LORA_G]
    u = w0_ref[...] + jnp.dot(jnp.tanh(xw).astype(BF16), w2_ref[...], preferred_element_type=F32)
    lw = -DECAY_SCALE * _sigmoid(u)
    a = _sigmoid(a0_ref[...] + jnp.dot(xa.astype(BF16), a2_ref[...], preferred_element_type=F32))
    g = jnp.dot(_sigmoid(xg).astype(BF16), g2_ref[...], preferred_element_type=F32)
    kk = k * kk_ref[...]
    ones_bd = _head_ones()
    kk2 = kk * kk
    ss = jnp.concatenate([_segsum(kk2[:, t * LANES:(t + 1) * LANES], ones_bd)
                          for t in range(d // LANES)], axis=1)
    kk = kk * lax.rsqrt(jnp.maximum(ss, 1e-24))
    kh = k * (1.0 + (a - 1.0) * ka_ref[...])
    r_o[...] = r.astype(r_o.dtype)
    kh_o[...] = kh.astype(kh_o.dtype)
    v_o[...] = v.astype(v_o.dtype)
    kk_o[...] = kk.astype(kk_o.dtype)
    ba_o[...] = (kk * a).astype(ba_o.dtype)
    lw_o[...] = lw
    g_o[...] = g.astype(g_o.dtype)


def _prep_prompt_kernel(zr_ref, zl_ref, mur, mul, w0, w2, a0, a2, g2, kkp, kap,
                        r_o, kh_o, v_o, kk_o, ba_o, lw_o, g_o, cr_ref, cl_ref):
    @pl.when(pl.program_id(1) == 0)
    def _():
        cr_ref[...] = jnp.zeros_like(cr_ref)
        cl_ref[...] = jnp.zeros_like(cl_ref)

    zr = zr_ref[...].astype(F32)
    zl = zl_ref[...].astype(F32)
    tt = zr.shape[0]
    first = lax.broadcasted_iota(jnp.int32, (tt, 1), 0) == 0
    zpr = jnp.where(first, cr_ref[0:1, :], pltpu.roll(zr, 1, 0))
    zpl = jnp.where(first, cl_ref[0:1, :], pltpu.roll(zl, 1, 0))
    cr_ref[0:1, :] = zr[tt - 1:tt, :]
    cl_ref[0:1, :] = zl[tt - 1:tt, :]
    _prep_core(zr, zpr, zl, zpl, mur, mul, w0, w2, a0, a2, g2, kkp, kap,
               (r_o, kh_o, v_o, kk_o, ba_o, lw_o, g_o))


def _prep_sample_kernel(zr_ref, zl_ref, pr_ref, pl_ref, mur, mul, w0, w2, a0, a2, g2, kkp, kap,
                        r_o, kh_o, v_o, kk_o, ba_o, lw_o, g_o):
    _prep_core(zr_ref[...].astype(F32), pr_ref[...], zl_ref[...].astype(F32), pl_ref[...], mur, mul, w0, w2, a0, a2, g2,
               kkp, kap, (r_o, kh_o, v_o, kk_o, ba_o, lw_o, g_o))


def _const_spec(shape):
    return pl.BlockSpec(shape, lambda *_: (0,) * len(shape))


def _prep_param_specs(d):
    return [_const_spec((1, 3 * d)), _const_spec((1, 512)), _const_spec((1, d)),
            _const_spec((LORA_PAD, d)), _const_spec((1, d)), _const_spec((LORA_PAD, d)),
            _const_spec((LORA_G, d)), _const_spec((1, d)), _const_spec((1, d))]


def _rwkv_prep_prompt(z, params, nb, t, d, tt, lora_blk):
    m = nb * t
    nt = t // tt
    row = lambda b, i: (b * nt + i, 0)
    act = jax.ShapeDtypeStruct((m, d), BF16)
    out_shape = [act] * 5 + [jax.ShapeDtypeStruct((m, d), F32), act]
    vmem = 2 * tt * (3 * d + 512) * 4 + 14 * tt * d * 4 + 24 * tt * d * 4 + (8 << 20)
    return pl.pallas_call(
        _prep_prompt_kernel,
        grid=(nb, nt),
        in_specs=[pl.BlockSpec((tt, 3 * d), row),
                  pl.BlockSpec((tt, 512), lambda b, i: (b * nt + i, lora_blk))]
                 + _prep_param_specs(d),
        out_specs=[pl.BlockSpec((tt, d), row)] * 7,
        out_shape=out_shape,
        scratch_shapes=[pltpu.VMEM((8, 3 * d), F32), pltpu.VMEM((8, 512), F32)],
        compiler_params=_cparams(("arbitrary", "arbitrary"), vmem),
        name="rwkv_prep_prompt",
    )(z, z, *params)


def _rwkv_prep_sample(z, prev_r, prev_l, params, d, lora_blk):
    m = z.shape[0]
    out = jax.ShapeDtypeStruct((m, d), F32)
    vmem = 4 * m * (3 * d + 512) * 4 + 14 * m * d * 4 + 24 * m * d * 4 + (8 << 20)
    return pl.pallas_call(
        _prep_sample_kernel,
        grid=(1,),
        in_specs=[pl.BlockSpec((m, 3 * d), lambda i: (0, 0)),
                  pl.BlockSpec((m, 512), lambda i: (0, lora_blk)),
                  _const_spec((m, 3 * d)), _const_spec((m, 512))]
                 + _prep_param_specs(d),
        out_specs=[pl.BlockSpec((m, d), lambda i: (0, 0))] * 7,
        out_shape=[out] * 7,
        compiler_params=_cparams(("arbitrary",), vmem),
        name="rwkv_prep_sample",
    )(z, z, prev_r, prev_l, *params)


def _rwkv_post(y, r, kh, v, g, rk, lnw, lnb, ones_bd, n_tiles):
    rows = y.shape[0] // n_tiles
    par = lambda a: jnp.concatenate(
        [jnp.broadcast_to(a[:, p * LANES:(p + 1) * LANES], (rows, LANES)) for p in range(n_tiles)], axis=0)
    inv = 1.0 / HEAD
    mean = _segsum(y, ones_bd) * inv
    dlt = y - mean
    var = _segsum(dlt * dlt, ones_bd) * inv
    yn = dlt * lax.rsqrt(var + GN_EPS) * par(lnw) + par(lnb)
    bonus = _segsum(r * kh * par(rk), ones_bd) * v
    return (yn + bonus) * g


def _pair_diag(x, lane_lo):
    return jnp.concatenate([jnp.where(lane_lo, x, 0.0), jnp.where(lane_lo, 0.0, x)], axis=0)


def _scan_kernel(r_ref, kh_ref, v_ref, kk_ref, ba_ref, lw_ref, g_ref, rk_ref, lnw_ref, lnb_ref,
                 o_ref, st_ref, s_scr):
    c = pl.program_id(1)
    n_pairs = s_scr.shape[0]
    C = CHUNK

    @pl.when(c == 0)
    def _():
        s_scr[...] = jnp.zeros_like(s_scr)

    row = lax.broadcasted_iota(jnp.int32, (C, LANES), 0)
    lane = lax.broadcasted_iota(jnp.int32, (C, LANES), 1)
    src = lane & (HEAD - 1)
    strict = src < row
    incl = src <= row
    eye_w = jnp.where(src == row, 1.0, 0.0)
    lane_lo = lane < HEAD
    sq_r = lax.broadcasted_iota(jnp.int32, (LANES, LANES), 0) >> 6
    sq_c = lax.broadcasted_iota(jnp.int32, (LANES, LANES), 1) >> 6
    same_head = sq_r == sq_c
    ones_bd = jnp.where(same_head, 1.0, 0.0).astype(BF16)
    tri = jnp.where(lax.broadcasted_iota(jnp.int32, (C, C), 1)
                    <= lax.broadcasted_iota(jnp.int32, (C, C), 0), 1.0, 0.0).astype(BF16)

    pairs = range(n_pairs)
    tile = lambda x, p: x[:, p * LANES:(p + 1) * LANES]
    rows_of = lambda x: jnp.concatenate([tile(x, p) for p in pairs], axis=0)
    lanes_of = lambda x: jnp.concatenate([x[p * C:(p + 1) * C] for p in pairs], axis=1)
    pd = lambda x: _pair_diag(x, lane_lo)

    def chunk(sub):
        rs = slice(sub * C, (sub + 1) * C)
        f32 = lambda ref: ref[rs, :].astype(F32)
        r, kh, v = f32(r_ref), f32(kh_ref), f32(v_ref)
        kk, ba, lw = f32(kk_ref), f32(ba_ref), lw_ref[rs, :]
        l_hi, l_mid, l_lo = _split3(lw)
        cs = (jnp.dot(tri, l_hi, preferred_element_type=F32)
              + jnp.dot(tri, l_mid, preferred_element_type=F32)
              + jnp.dot(tri, l_lo, preferred_element_type=F32))
        c_end = cs[C - 1:C, :]
        e_neg = jnp.exp(-cs)
        e_end = jnp.exp(c_end - cs)
        r_t = r * jnp.exp(cs)
        k_x = kk * jnp.exp(cs - lw)
        k_t = kh * e_neg
        b_t = ba * e_neg
        k_e = kh * e_end
        b_e = ba * e_end
        g_end = jnp.exp(c_end)

        gram = [_dot_nt(jnp.concatenate([tile(k_x, p), tile(r_t, p)], axis=0),
                        jnp.concatenate([pd(tile(k_t, p)), pd(tile(b_t, p))], axis=0)) for p in pairs]
        a_kk = [jnp.where(strict, gm[0:C, 0:LANES], 0.0) for gm in gram]
        a_kb = [jnp.where(strict, gm[0:C, LANES:2 * LANES], 0.0) for gm in gram]
        a_rk = [jnp.where(incl, gm[C:2 * C, 0:LANES], 0.0) for gm in gram]
        a_rb = [jnp.where(incl, gm[C:2 * C, LANES:2 * LANES], 0.0) for gm in gram]

        pw = [_dot(a, pd(a)) for a in a_kb]
        t_inv = [eye_w - a for a in a_kb]
        n_lvl = int(math.log2(C)) - 1
        for lvl in range(n_lvl):
            last = lvl == n_lvl - 1
            prod = [_dot(t_inv[p] if last else jnp.concatenate([t_inv[p], pw[p]], axis=0), pd(pw[p]))
                    for p in pairs]
            t_inv = [t_inv[p] + prod[p][0:C] for p in pairs]
            if not last:
                pw = [prod[p][C:2 * C] for p in pairs]

        s_bd = [s_scr[p] for p in pairs]
        v_bd = [pd(tile(v, p)) for p in pairs]
        w_mat = [_dot_nt(tile(k_x, p), s_bd[p]) + _dot(a_kk[p], v_bd[p]) for p in pairs]
        u = [_dot(t_inv[p], pd(w_mat[p])) for p in pairs]
        y = [_dot_nt(tile(r_t, p), s_bd[p]) + _dot(a_rk[p], v_bd[p]) - _dot(a_rb[p], pd(u[p]))
             for p in pairs]
        for p in pairs:
            upd = _dot_tn(jnp.concatenate([tile(v, p), u[p]], axis=0),
                          jnp.concatenate([tile(k_e, p), -tile(b_e, p)], axis=0))
            s_scr[p] = s_bd[p] * tile(g_end, p) + jnp.where(same_head, upd, 0.0)

        y_rows = jnp.concatenate(y, axis=0)
        out = _rwkv_post(y_rows, rows_of(r), rows_of(kh), rows_of(v), rows_of(f32(g_ref)),
                         rk_ref[...], lnw_ref[...], lnb_ref[...], ones_bd, n_pairs)
        o_ref[rs, :] = lanes_of(out)

    for sub in range(r_ref.shape[0] // C):
        chunk(sub)

    @pl.when(c == pl.num_programs(1) - 1)
    def _():
        for p in range(n_pairs):
            s = s_scr[p]
            st_ref[0, 2 * p] = s[0:HEAD, 0:HEAD]
            st_ref[0, 2 * p + 1] = s[HEAD:2 * HEAD, HEAD:2 * HEAD]


def _rwkv_scan(r, kh, v, kk, ba, lw, g, rk, lnw, lnb, nb, t, d):
    rows = CHUNK * SCAN_CHUNKS_PER_STEP
    nc = t // rows
    nh = d // HEAD
    blk = pl.BlockSpec((rows, d), lambda b, c: (b * nc + c, 0))
    par = _const_spec((1, d))
    vmem = 16 * rows * d * 4 + 3 * (d // LANES) * LANES * LANES * 4 + (16 << 20)
    return pl.pallas_call(
        _scan_kernel,
        grid=(nb, nc),
        in_specs=[blk] * 7 + [par] * 3,
        out_specs=[blk, pl.BlockSpec((1, nh, HEAD, HEAD), lambda b, c: (b, 0, 0, 0))],
        out_shape=[jax.ShapeDtypeStruct((nb * t, d), F32),
                   jax.ShapeDtypeStruct((nb, nh, HEAD, HEAD), F32)],
        scratch_shapes=[pltpu.VMEM((d // LANES, LANES, LANES), F32)],
        compiler_params=_cparams(("arbitrary", "arbitrary"), vmem),
        name="rwkv_scan",
    )(r, kh, v, kk, ba, lw, g, rk, lnw, lnb)


def _step_kernel(s_ref, r_ref, kh_ref, v_ref, kk_ref, ba_ref, lw_ref, g_ref, rk_ref, lnw_ref,
                 lnb_ref, so_ref, o_ref):
    bb, nh = r_ref.shape[0], r_ref.shape[1]
    eye = (lax.broadcasted_iota(jnp.int32, (HEAD, HEAD), 0)
           == lax.broadcasted_iota(jnp.int32, (HEAD, HEAD), 1))
    inv = 1.0 / HEAD

    def body(b, carry):
        r_b, kh_b, v_b = r_ref[b], kh_ref[b], v_ref[b]
        kk_b, ba_b, dec_b = kk_ref[b], ba_ref[b], jnp.exp(lw_ref[b])
        rows = []
        for h in range(nh):
            s = s_ref[b, h]
            sa = -jnp.sum(s * kk_b[h:h + 1], axis=-1, keepdims=True)
            v_col = jnp.sum(jnp.where(eye, v_b[h:h + 1], 0.0), axis=-1, keepdims=True)
            s_new = s * dec_b[h:h + 1] + sa * ba_b[h:h + 1] + v_col * kh_b[h:h + 1]
            so_ref[b, h] = s_new
            y_col = jnp.sum(s_new * r_b[h:h + 1], axis=-1, keepdims=True)
            rows.append(jnp.sum(jnp.where(eye, y_col, 0.0), axis=0, keepdims=True))
        y = jnp.concatenate(rows, axis=0)
        mean = jnp.sum(y, axis=-1, keepdims=True) * inv
        dlt = y - mean
        var = jnp.sum(dlt * dlt, axis=-1, keepdims=True) * inv
        yn = dlt * lax.rsqrt(var + GN_EPS) * lnw_ref[...] + lnb_ref[...]
        bonus = jnp.sum(r_b * kh_b * rk_ref[...], axis=-1, keepdims=True) * v_b
        o_ref[b] = (yn + bonus) * g_ref[b]
        return carry

    lax.fori_loop(0, bb, body, 0)


def _rwkv_step(state, r, kh, v, kk, ba, lw, g, rk, lnw, lnb, bb):
    nb, nh = state.shape[0], state.shape[1]
    vec = pl.BlockSpec((bb, nh, HEAD), lambda i: (i, 0, 0))
    st = pl.BlockSpec((bb, nh, HEAD, HEAD), lambda i: (i, 0, 0, 0))
    par = _const_spec((nh, HEAD))
    vmem = 4 * bb * nh * HEAD * LANES * 4 + (16 << 20)
    return pl.pallas_call(
        _step_kernel,
        grid=(nb // bb,),
        in_specs=[st] + [vec] * 7 + [par] * 3,
        out_specs=[st, vec],
        out_shape=[jax.ShapeDtypeStruct(state.shape, F32),
                   jax.ShapeDtypeStruct((nb, nh, HEAD), F32)],
        compiler_params=_cparams(("arbitrary",), vmem),
        name="rwkv_step",
    )(state, r, kh, v, kk, ba, lw, g, rk, lnw, lnb)


def _alibi_slope(h, n_heads):
    return 2.0 ** (-8.0 * (h + 1) / n_heads)


def _attn_prompt_kernel(sink_ref, q_ref, kp_ref, kc_ref, vp_ref, vc_ref, o_ref):
    n = pl.program_id(1)
    nq = q_ref.shape[1] // HEAD
    group = nq // N_KV
    q = q_ref[...] * SCALE
    k2 = jnp.concatenate([kp_ref[...], kc_ref[...]], axis=0).astype(BF16)
    v2 = jnp.concatenate([vp_ref[...], vc_ref[...]], axis=0).astype(BF16)
    qi = lax.broadcasted_iota(jnp.int32, (WINDOW, 2 * WINDOW), 0)
    si = lax.broadcasted_iota(jnp.int32, (WINDOW, 2 * WINDOW), 1)
    dist = qi + WINDOW - si
    first_key = jnp.where(n > 0, 0, WINDOW)
    valid = (dist >= 0) & (dist < WINDOW) & (si >= first_key)
    dist_m = jnp.where(valid, dist.astype(F32), jnp.inf)
    outs = []
    for h in range(nq):
        j = h // group
        qh = q[:, h * HEAD:(h + 1) * HEAD]
        s = _dot_nt(qh, k2[:, j * HEAD:(j + 1) * HEAD]) - _alibi_slope(h, nq) * dist_m
        sink = sink_ref[h]
        mx = jnp.maximum(jnp.max(s, axis=-1, keepdims=True), sink)
        p = jnp.exp(s - mx)
        den = jnp.sum(p, axis=-1, keepdims=True) + jnp.exp(sink - mx)
        outs.append(_dot(p, v2[:, j * HEAD:(j + 1) * HEAD]) / den)
    o_ref[...] = jnp.concatenate(outs, axis=1)


def _attn_prompt(z, sinks, nb, t, dq, q_blk, k_blk, v_blk):
    nblk = t // WINDOW
    dkv = N_KV * HEAD
    cur = lambda col: (lambda b, n: (b * nblk + n, col))
    prev = lambda col: (lambda b, n: (b * nblk + jnp.maximum(n - 1, 0), col))
    vmem = 4 * WINDOW * (dq * 2 + 4 * dkv) * 4 + (24 << 20)
    return pl.pallas_call(
        _attn_prompt_kernel,
        grid=(nb, nblk),
        in_specs=[pl.BlockSpec(memory_space=pltpu.SMEM),
                  pl.BlockSpec((WINDOW, dq), cur(q_blk)),
                  pl.BlockSpec((WINDOW, dkv), prev(k_blk)),
                  pl.BlockSpec((WINDOW, dkv), cur(k_blk)),
                  pl.BlockSpec((WINDOW, dkv), prev(v_blk)),
                  pl.BlockSpec((WINDOW, dkv), cur(v_blk))],
        out_specs=pl.BlockSpec((WINDOW, dq), lambda b, n: (b * nblk + n, 0)),
        out_shape=jax.ShapeDtypeStruct((nb * t, dq), F32),
        compiler_params=_cparams(("arbitrary", "arbitrary"), vmem),
        name="attn_prompt",
    )(sinks, z, z, z, z, z)


def _attn_sample_kernel(q_ref, kn_ref, vn_ref, ck_ref, cv_ref, sink_ref, slope_ref,
                        o_ref, ko_ref, vo_ref):
    bb, nq = q_ref.shape[0], q_ref.shape[1]
    group = nq // N_KV
    dkv = N_KV * HEAD
    hrow = lax.broadcasted_iota(jnp.int32, (nq, dkv), 0) // group
    hlane = lax.broadcasted_iota(jnp.int32, (nq, dkv), 1) >> 6
    own = hrow == hlane
    si = lax.broadcasted_iota(jnp.int32, (nq, WINDOW), 1)
    distf = (WINDOW - si).astype(F32)
    valid = si >= 1
    sink = sink_ref[...]
    slope = slope_ref[...]

    def body(b, carry):
        q = q_ref[b]
        q_bd = jnp.where(own, jnp.concatenate([q] * N_KV, axis=1), 0.0)
        ck = ck_ref[b]
        cv = cv_ref[b]
        kn = kn_ref[b]
        vn = vn_ref[b]
        s_c = _dot_nt(q_bd, ck) * SCALE - slope * distf
        s_c = jnp.where(valid, s_c, -jnp.inf)
        s_n = jnp.sum(q_bd * kn, axis=-1, keepdims=True) * SCALE
        mx = jnp.maximum(jnp.maximum(jnp.max(s_c, axis=-1, keepdims=True), s_n), sink)
        p_c = jnp.exp(s_c - mx)
        p_n = jnp.exp(s_n - mx)
        den = jnp.sum(p_c, axis=-1, keepdims=True) + p_n + jnp.exp(sink - mx)
        o_all = (_dot(p_c, cv) + p_n * vn) / den
        o_ref[b] = jnp.concatenate(
            [o_all[j * group:(j + 1) * group, j * HEAD:(j + 1) * HEAD] for j in range(N_KV)], axis=0)
        ko_ref[b] = jnp.concatenate([ck[1:], kn], axis=0)
        vo_ref[b] = jnp.concatenate([cv[1:], vn], axis=0)
        return carry

    lax.fori_loop(0, bb, body, 0)


def _attn_sample(q, kn, vn, ck, cv, sinks, slopes, bb):
    nb, nq = q.shape[0], q.shape[1]
    dkv = N_KV * HEAD
    cache = pl.BlockSpec((bb, WINDOW, dkv), lambda i: (i, 0, 0))
    new = pl.BlockSpec((bb, 1, dkv), lambda i: (i, 0, 0))
    qs = pl.BlockSpec((bb, nq, HEAD), lambda i: (i, 0, 0))
    vmem = 8 * bb * WINDOW * dkv * 4 + (16 << 20)
    return pl.pallas_call(
        _attn_sample_kernel,
        grid=(nb // bb,),
        in_specs=[qs, new, new, cache, cache, _const_spec((nq, 1)), _const_spec((nq, 1))],
        out_specs=[qs, cache, cache],
        out_shape=[jax.ShapeDtypeStruct((nb, nq, HEAD), F32),
                   jax.ShapeDtypeStruct(ck.shape, F32),
                   jax.ShapeDtypeStruct(cv.shape, F32)],
        compiler_params=_cparams(("arbitrary",), vmem),
        name="attn_sample",
    )(q, kn, vn, ck, cv, sinks, slopes)


def _out_proj_kernel(a_ref, b_ref, x_ref, w_ref, o_ref):
    d = a_ref.shape[1]
    o_ref[...] = (x_ref[...]
                  + jnp.dot(a_ref[...].astype(BF16), w_ref[0:d, :], preferred_element_type=F32)
                  + jnp.dot(b_ref[...].astype(BF16), w_ref[d:, :], preferred_element_type=F32))


def _out_proj(a, b, x, w, tm):
    m, d = a.shape
    dm = x.shape[1]
    vmem = 4 * tm * d * 4 + 4 * tm * dm * 4 + 2 * 2 * d * dm * 2 + (8 << 20)
    return pl.pallas_call(
        _out_proj_kernel,
        grid=(m // tm,),
        in_specs=[pl.BlockSpec((tm, d), lambda i: (i, 0)),
                  pl.BlockSpec((tm, d), lambda i: (i, 0)),
                  pl.BlockSpec((tm, dm), lambda i: (i, 0)),
                  _const_spec((2 * d, dm))],
        out_specs=pl.BlockSpec((tm, dm), lambda i: (i, 0)),
        out_shape=jax.ShapeDtypeStruct((m, dm), F32),
        compiler_params=_cparams(("arbitrary",), vmem),
        name="out_proj",
    )(a, b, x, w)


def _ffn_kernel(x_ref, g_ref, wg_ref, wu_ref, wd_ref, o_ref, h_ref, acc_ref):
    j = pl.program_id(1)

    @pl.when(j == 0)
    def _():
        h_ref[...] = _rms(x_ref[...], g_ref[...]).astype(BF16)
        acc_ref[...] = jnp.zeros_like(acc_ref)

    h = h_ref[...]
    gate = jnp.dot(h, wg_ref[...], preferred_element_type=F32)
    up = jnp.dot(h, wu_ref[...], preferred_element_type=F32)
    act = gate * _sigmoid(gate) * up
    acc_ref[...] += jnp.dot(act.astype(BF16), wd_ref[...], preferred_element_type=F32)

    @pl.when(j == pl.num_programs(1) - 1)
    def _():
        o_ref[...] = x_ref[...] + acc_ref[...]


def _ffn(x, gain, wg, wu, wd, tm, tf):
    m, dm = x.shape
    f = wg.shape[1]
    vmem = 4 * tm * dm * 4 + tm * dm * 2 + tm * dm * 4 + 6 * dm * tf * 2 + 4 * tm * tf * 4 + (6 << 20)
    return pl.pallas_call(
        _ffn_kernel,
        grid=(m // tm, f // tf),
        in_specs=[pl.BlockSpec((tm, dm), lambda i, j: (i, 0)),
                  _const_spec((1, dm)),
                  pl.BlockSpec((dm, tf), lambda i, j: (0, j)),
                  pl.BlockSpec((dm, tf), lambda i, j: (0, j)),
                  pl.BlockSpec((tf, dm), lambda i, j: (j, 0))],
        out_specs=pl.BlockSpec((tm, dm), lambda i, j: (i, 0)),
        out_shape=jax.ShapeDtypeStruct((m, dm), F32),
        scratch_shapes=[pltpu.VMEM((tm, dm), BF16), pltpu.VMEM((tm, dm), F32)],
        compiler_params=_cparams(("arbitrary", "arbitrary"), vmem),
        name="ffn",
    )(x, gain, wg, wu, wd)


def _ple_kernel(x_ref, p_ref, gn_ref, wgate_ref, wproj_ref, fn_ref, o_ref):
    x = x_ref[...]
    gate = _sigmoid(jnp.dot(_rms(x, gn_ref[...]).astype(BF16), wgate_ref[...],
                            preferred_element_type=F32))
    emb = jnp.dot(p_ref[...].astype(BF16), wproj_ref[...], preferred_element_type=F32)
    o_ref[...] = _rms(x + gate * emb, fn_ref[...])


def _ple_final(x, p, gn, wgate, wproj, fn, tm):
    m, dm = x.shape
    dp = p.shape[1]
    vmem = 4 * tm * dm * 4 + 2 * tm * dp * 4 + 2 * dm * dm * 2 + 2 * dp * dm * 2 + 4 * tm * dm * 4 + (6 << 20)
    return pl.pallas_call(
        _ple_kernel,
        grid=(m // tm,),
        in_specs=[pl.BlockSpec((tm, dm), lambda i: (i, 0)),
                  pl.BlockSpec((tm, dp), lambda i: (i, 0)),
                  _const_spec((1, dm)), _const_spec((dm, dm)), _const_spec((dp, dm)),
                  _const_spec((1, dm))],
        out_specs=pl.BlockSpec((tm, dm), lambda i: (i, 0)),
        out_shape=jax.ShapeDtypeStruct((m, dm), F32),
        compiler_params=_cparams(("arbitrary",), vmem),
        name="ple_final",
    )(x, p, gn, wgate, wproj, fn)


def _pad_cols(a, n):
    return jnp.pad(a, ((0, 0), (0, n - a.shape[1])))


def _layer(x, p_emb, lw_, final_norm, *, nb, t, state, shift_prev, caches):
    m, dm = x.shape
    d = lw_["d"]
    dq = lw_["dq"]
    dkv = N_KV * HEAD
    sample = caches is not None
    o_q, o_k, o_v, o_l = 3 * d, 3 * d + dq, 3 * d + dq + dkv, 3 * d + dq + 2 * dkv
    tm = min(m, 1024)
    z = _norm_matmul(x, lw_["norm_mix"], lw_["w_in"], tm, 1280)
    lora_blk = o_l // 512
    nh = d // HEAD
    if not sample:
        feats = _rwkv_prep_prompt(z, lw_["prep"], nb, t, d, 256, lora_blk)
        out_r, wkv = _rwkv_scan(*feats, lw_["r_k"], lw_["ln_w"], lw_["ln_b"], nb, t, d)
        y_a = _attn_prompt(z, lw_["sinks"], nb, t, dq, o_q // dq, o_k // dkv, o_v // dkv)
        z3 = z.reshape(nb, t, -1)
        shift_rows = z3[:, -1].astype(F32)
        k_new = z3[:, -WINDOW:, o_k:o_k + dkv].astype(F32).reshape(nb, WINDOW, N_KV, HEAD)
        v_new = z3[:, -WINDOW:, o_v:o_v + dkv].astype(F32).reshape(nb, WINDOW, N_KV, HEAD)
    else:
        prev_r = shift_prev[:, :3 * d]
        prev_l = jnp.concatenate(
            [_pad_cols(shift_prev[:, 3 * d:3 * d + LORA_W], LORA_PAD),
             _pad_cols(shift_prev[:, 3 * d + LORA_W:3 * d + LORA_W + LORA_A], LORA_PAD),
             shift_prev[:, 3 * d + LORA_W + LORA_A:]], axis=1)
        feats = _rwkv_prep_sample(z, prev_r, prev_l, lw_["prep"], d, lora_blk)
        feats3 = [f.reshape(m, nh, HEAD) for f in feats]
        wkv, out_r3 = _rwkv_step(state, *feats3, lw_["r_k"].reshape(nh, HEAD),
                                 lw_["ln_w"].reshape(nh, HEAD), lw_["ln_b"].reshape(nh, HEAD), 8)
        out_r = out_r3.reshape(m, d)
        ck, cv = caches
        q3 = z[:, o_q:o_q + dq].astype(F32).reshape(m, dq // HEAD, HEAD)
        kn = z[:, o_k:o_k + dkv].astype(F32).reshape(m, 1, dkv)
        vn = z[:, o_v:o_v + dkv].astype(F32).reshape(m, 1, dkv)
        y3, k_new, v_new = _attn_sample(q3, kn, vn, ck.reshape(m, WINDOW, dkv),
                                        cv.reshape(m, WINDOW, dkv), lw_["sinks"].reshape(-1, 1),
                                        lw_["slopes"], 8)
        y_a = y3.reshape(m, dq)
        k_new = k_new.reshape(m, WINDOW, N_KV, HEAD)
        v_new = v_new.reshape(m, WINDOW, N_KV, HEAD)
        shift_rows = z.astype(F32)
    shift_new = jnp.concatenate(
        [shift_rows[:, :3 * d], shift_rows[:, o_l:o_l + LORA_W],
         shift_rows[:, o_l + LORA_PAD:o_l + LORA_PAD + LORA_A],
         shift_rows[:, o_l + 2 * LORA_PAD:o_l + 2 * LORA_PAD + LORA_G]], axis=1)
    tm2 = min(m, 512)
    x1 = _out_proj(out_r, y_a, x, lw_["w_out"], tm2)
    x2 = _ffn(x1, lw_["norm_ffn"], lw_["w_gate"], lw_["w_up"], lw_["w_down"], tm2, 512)
    y = _ple_final(x2, p_emb, lw_["norm_ple"], lw_["ple_gate"], lw_["ple_proj"], final_norm, tm2)
    return y, wkv, shift_new, k_new, v_new


def kernel(x_prompt, x_sample, state_wkv, state_shift, cache_k, cache_v, p_prompt, p_sample, norm_mix, w_in, mu_shift, rwkv_w0, rwkv_w2, rwkv_a0, rwkv_a2, rwkv_g2, rwkv_k_k, rwkv_k_a, rwkv_r_k, rwkv_ln_w, rwkv_ln_b, attn_sinks, w_out, norm_ffn, w_gate, w_up, w_down, norm_ple, ple_gate, ple_proj, final_norm):
    depth = w_in.shape[0]
    assert depth == 1, "single-layer step"
    bp, t, dm = x_prompt.shape
    bs = x_sample.shape[0]
    assert x_sample.shape[1] == 1
    d = rwkv_w0.shape[1]
    dq = attn_sinks.shape[1] * HEAD
    dkv = N_KV * HEAD
    i = 0
    row = lambda a: a.reshape(1, -1)
    wi = w_in[i]
    rp = 3 * d + LORA_W + LORA_A + LORA_G
    w_in_p = jnp.concatenate(
        [wi[:, :3 * d], wi[:, rp:],
         _pad_cols(wi[:, 3 * d:3 * d + LORA_W], LORA_PAD),
         _pad_cols(wi[:, 3 * d + LORA_W:3 * d + LORA_W + LORA_A], LORA_PAD),
         wi[:, 3 * d + LORA_W + LORA_A:rp]], axis=1).astype(BF16)
    mu = row(mu_shift[i])
    mu_r = mu[:, :3 * d]
    mu_l = jnp.concatenate(
        [_pad_cols(mu[:, 3 * d:3 * d + LORA_W], LORA_PAD),
         _pad_cols(mu[:, 3 * d + LORA_W:3 * d + LORA_W + LORA_A], LORA_PAD),
         mu[:, 3 * d + LORA_W + LORA_A:]], axis=1)
    pad_rows = lambda a: jnp.pad(a, ((0, LORA_PAD - a.shape[0]), (0, 0))).astype(BF16)
    nq = dq // HEAD
    lw_ = {
        "d": d, "dq": dq,
        "norm_mix": row(norm_mix[i]), "w_in": w_in_p,
        "prep": (mu_r, mu_l, row(rwkv_w0[i]), pad_rows(rwkv_w2[i]), row(rwkv_a0[i]),
                 pad_rows(rwkv_a2[i]), rwkv_g2[i].astype(BF16), row(rwkv_k_k[i]), row(rwkv_k_a[i])),
        "r_k": row(rwkv_r_k[i]), "ln_w": row(rwkv_ln_w[i]), "ln_b": row(rwkv_ln_b[i]),
        "sinks": attn_sinks[i],
        "slopes": (2.0 ** (-8.0 * jnp.arange(1, nq + 1, dtype=F32) / nq)).reshape(nq, 1),
        "w_out": w_out[i].astype(BF16), "norm_ffn": row(norm_ffn[i]),
        "w_gate": w_gate[i].astype(BF16), "w_up": w_up[i].astype(BF16),
        "w_down": w_down[i].astype(BF16), "norm_ple": row(norm_ple[i]),
        "ple_gate": ple_gate[i].astype(BF16), "ple_proj": ple_proj[i].astype(BF16),
    }
    fn = row(final_norm)
    yp, wkv_p, shift_p, k_p, v_p = _layer(
        x_prompt.reshape(bp * t, dm), p_prompt[i].reshape(bp * t, -1), lw_, fn,
        nb=bp, t=t, state=None, shift_prev=None, caches=None)
    ys, wkv_s, shift_s, k_s, v_s = _layer(
        x_sample.reshape(bs, dm), p_sample[i].reshape(bs, -1), lw_, fn,
        nb=bs, t=1, state=state_wkv[i], shift_prev=state_shift[i],
        caches=(cache_k[i], cache_v[i]))
    return (yp.reshape(bp, t, dm), ys.reshape(bs, 1, dm),
            wkv_p[None], shift_p[None], k_p[None], v_p[None],
            wkv_s[None], shift_s[None], k_s[None], v_s[None])
```

```python
import functools
import math

import jax
import jax.numpy as jnp
from jax import lax
from jax.experimental import pallas as pl
from jax.experimental.pallas import tpu as pltpu

F32 = jnp.float32
BF16 = jnp.bfloat16

HEAD = 64
LANES = 128
N_KV = 4
WINDOW = 128
LORA_W = 96
LORA_A = 96
LORA_G = 256
LORA_PAD = 128
EPS = 1e-6
GN_EPS = 64e-5
SCALE = HEAD ** -0.5
assert math.log2(SCALE).is_integer()
CHUNK = 64
SCAN_CHUNKS_PER_STEP = 2
DECAY_SCALE = math.exp(-0.5)
VMEM_CAP = 60 * 1024 * 1024


def _cparams(sem, vmem_bytes):
    return pltpu.CompilerParams(dimension_semantics=sem,
                                vmem_limit_bytes=int(min(VMEM_CAP, vmem_bytes)))


def _dot(a, b):
    return jnp.dot(a.astype(BF16), b.astype(BF16), preferred_element_type=F32)


def _dot_nt(a, b):
    return lax.dot_general(a.astype(BF16), b.astype(BF16), (((1,), (1,)), ((), ())),
                           preferred_element_type=F32)


def _dot_tn(a, b):
    return lax.dot_general(a.astype(BF16), b.astype(BF16), (((0,), (0,)), ((), ())),
                           preferred_element_type=F32)


def _split2(x):
    hi = x.astype(BF16)
    lo = (x - hi.astype(F32)).astype(BF16)
    return hi, lo


def _split3(x):
    hi = x.astype(BF16)
    r1 = x - hi.astype(F32)
    mid = r1.astype(BF16)
    lo = (r1 - mid.astype(F32)).astype(BF16)
    return hi, mid, lo


def _head_ones():
    i = lax.broadcasted_iota(jnp.int32, (LANES, LANES), 0) >> 6
    j = lax.broadcasted_iota(jnp.int32, (LANES, LANES), 1) >> 6
    return jnp.where(i == j, 1.0, 0.0).astype(BF16)


def _segsum(x, ones_bd):
    hi, lo = _split2(x)
    return (jnp.dot(hi, ones_bd, preferred_element_type=F32)
            + jnp.dot(lo, ones_bd, preferred_element_type=F32))


def _rms(x, g):
    ms = jnp.mean(x * x, axis=-1, keepdims=True)
    return x * lax.rsqrt(ms + EPS) * g


def _sigmoid(x):
    return 1.0 / (1.0 + jnp.exp(-x))


def _norm_mm_kernel(x_ref, g_ref, w_ref, o_ref, h_ref):
    @pl.when(pl.program_id(1) == 0)
    def _():
        h_ref[...] = _rms(x_ref[...], g_ref[...]).astype(BF16)

    o_ref[...] = jnp.dot(h_ref[...], w_ref[...], preferred_element_type=F32).astype(o_ref.dtype)


def _norm_matmul(x, gain, w, tm, tn):
    m, k = x.shape
    n = w.shape[1]
    vmem = 2 * tm * k * 4 + 2 * k * tn * 2 + 2 * tm * tn * 4 + tm * k * 2 + (4 << 20)
    return pl.pallas_call(
        _norm_mm_kernel,
        grid=(m // tm, n // tn),
        in_specs=[pl.BlockSpec((tm, k), lambda i, j: (i, 0)),
                  pl.BlockSpec((1, k), lambda i, j: (0, 0)),
                  pl.BlockSpec((k, tn), lambda i, j: (0, j))],
        out_specs=pl.BlockSpec((tm, tn), lambda i, j: (i, j)),
        out_shape=jax.ShapeDtypeStruct((m, n), BF16),
        scratch_shapes=[pltpu.VMEM((tm, k), BF16)],
        compiler_params=_cparams(("arbitrary", "arbitrary"), vmem),
        name="norm_in_proj",
    )(x, gain, w)


def _prep_core(zr, zpr, zl, zpl, mur_ref, mul_ref, w0_ref, w2_ref, a0_ref, a2_ref, g2_ref,
               kk_ref, ka_ref, out_refs):
    r_o, kh_o, v_o, kk_o, ba_o, lw_o, g_o = out_refs
    d = r_o.shape[-1]
    xr = zr + (zpr - zr) * mur_ref[...]
    xl = zl + (zpl - zl) * mul_ref[...]
    r = xr[:, 0:d]
    k = xr[:, d:2 * d]
    v = xr[:, 2 * d:3 * d]
    xw = xl[:, 0:LORA_PAD]
    xa = xl[:, LORA_PAD:2 * LORA_PAD]
    xg = xl[:, 2 * LORA_PAD:2 * LORA_PAD + LORA_G]
    u = w0_ref[...] + jnp.dot(jnp.tanh(xw).astype(BF16), w2_ref[...], preferred_element_type=F32)
    lw = -DECAY_SCALE * _sigmoid(u)
    a = _sigmoid(a0_ref[...] + jnp.dot(xa.astype(BF16), a2_ref[...], preferred_element_type=F32))
    g = jnp.dot(_sigmoid(xg).astype(BF16), g2_ref[...], preferred_element_type=F32)
    kk = k * kk_ref[...]
    ones_bd = _head_ones()
    kk2 = kk * kk
    ss = jnp.concatenate([_segsum(kk2[:, t * LANES:(t + 1) * LANES], ones_bd)
                          for t in range(d // LANES)], axis=1)
    kk = kk * lax.rsqrt(jnp.maximum(ss, 1e-24))
    kh = k * (1.0 + (a - 1.0) * ka_ref[...])
    r_o[...] = r.astype(r_o.dtype)
    kh_o[...] = kh.astype(kh_o.dtype)
    v_o[...] = v.astype(v_o.dtype)
    kk_o[...] = kk.astype(kk_o.dtype)
    ba_o[...] = (kk * a).astype(ba_o.dtype)
    lw_o[...] = lw
    g_o[...] = g.astype(g_o.dtype)


def _prep_prompt_kernel(zr_ref, zl_ref, mur, mul, w0, w2, a0, a2, g2, kkp, kap,
                        r_o, kh_o, v_o, kk_o, ba_o, lw_o, g_o, cr_ref, cl_ref):
    @pl.when(pl.program_id(1) == 0)
    def _():
        cr_ref[...] = jnp.zeros_like(cr_ref)
        cl_ref[...] = jnp.zeros_like(cl_ref)

    zr = zr_ref[...].astype(F32)
    zl = zl_ref[...].astype(F32)
    tt = zr.shape[0]
    first = lax.broadcasted_iota(jnp.int32, (tt, 1), 0) == 0
    zpr = jnp.where(first, cr_ref[0:1, :], pltpu.roll(zr, 1, 0))
    zpl = jnp.where(first, cl_ref[0:1, :], pltpu.roll(zl, 1, 0))
    cr_ref[0:1, :] = zr[tt - 1:tt, :]
    cl_ref[0:1, :] = zl[tt - 1:tt, :]
    _prep_core(zr, zpr, zl, zpl, mur, mul, w0, w2, a0, a2, g2, kkp, kap,
               (r_o, kh_o, v_o, kk_o, ba_o, lw_o, g_o))


def _prep_sample_kernel(zr_ref, zl_ref, pr_ref, pl_ref, mur, mul, w0, w2, a0, a2, g2, kkp, kap,
                        r_o, kh_o, v_o, kk_o, ba_o, lw_o, g_o):
    _prep_core(zr_ref[...].astype(F32), pr_ref[...], zl_ref[...].astype(F32), pl_ref[...], mur, mul, w0, w2, a0, a2, g2,
               kkp, kap, (r_o, kh_o, v_o, kk_o, ba_o, lw_o, g_o))


def _const_spec(shape):
    return pl.BlockSpec(shape, lambda *_: (0,) * len(shape))


def _prep_param_specs(d):
    return [_const_spec((1, 3 * d)), _const_spec((1, 512)), _const_spec((1, d)),
            _const_spec((LORA_PAD, d)), _const_spec((1, d)), _const_spec((LORA_PAD, d)),
            _const_spec((LORA_G, d)), _const_spec((1, d)), _const_spec((1, d))]


def _rwkv_prep_prompt(z, params, nb, t, d, tt, lora_blk):
    m = nb * t
    nt = t // tt
    row = lambda b, i: (b * nt + i, 0)
    act = jax.ShapeDtypeStruct((m, d), BF16)
    out_shape = [act] * 5 + [jax.ShapeDtypeStruct((m, d), F32), act]
    vmem = 2 * tt * (3 * d + 512) * 4 + 14 * tt * d * 4 + 24 * tt * d * 4 + (8 << 20)
    return pl.pallas_call(
        _prep_prompt_kernel,
        grid=(nb, nt),
        in_specs=[pl.BlockSpec((tt, 3 * d), row),
                  pl.BlockSpec((tt, 512), lambda b, i: (b * nt + i, lora_blk))]
                 + _prep_param_specs(d),
        out_specs=[pl.BlockSpec((tt, d), row)] * 7,
        out_shape=out_shape,
        scratch_shapes=[pltpu.VMEM((8, 3 * d), F32), pltpu.VMEM((8, 512), F32)],
        compiler_params=_cparams(("arbitrary", "arbitrary"), vmem),
        name="rwkv_prep_prompt",
    )(z, z, *params)


def _rwkv_prep_sample(z, prev_r, prev_l, params, d, lora_blk):
    m = z.shape[0]
    out = jax.ShapeDtypeStruct((m, d), F32)
    vmem = 4 * m * (3 * d + 512) * 4 + 14 * m * d * 4 + 24 * m * d * 4 + (8 << 20)
    return pl.pallas_call(
        _prep_sample_kernel,
        grid=(1,),
        in_specs=[pl.BlockSpec((m, 3 * d), lambda i: (0, 0)),
                  pl.BlockSpec((m, 512), lambda i: (0, lora_blk)),
                  _const_spec((m, 3 * d)), _const_spec((m, 512))]
                 + _prep_param_specs(d),
        out_specs=[pl.BlockSpec((m, d), lambda i: (0, 0))] * 7,
        out_shape=[out] * 7,
        compiler_params=_cparams(("arbitrary",), vmem),
        name="rwkv_prep_sample",
    )(z, z, prev_r, prev_l, *params)


def _rwkv_post(y, r, kh, v, g, rk, lnw, lnb, ones_bd, n_tiles):
    rows = y.shape[0] // n_tiles
    par = lambda a: jnp.concatenate(
        [jnp.broadcast_to(a[:, p * LANES:(p + 1) * LANES], (rows, LANES)) for p in range(n_tiles)], axis=0)
    inv = 1.0 / HEAD
    mean = _segsum(y, ones_bd) * inv
    dlt = y - mean
    var = _segsum(dlt * dlt, ones_bd) * inv
    yn = dlt * lax.rsqrt(var + GN_EPS) * par(lnw) + par(lnb)
    bonus = _segsum(r * kh * par(rk), ones_bd) * v
    return (yn + bonus) * g


def _pair_diag(x, lane_lo):
    return jnp.concatenate([jnp.where(lane_lo, x, 0.0), jnp.where(lane_lo, 0.0, x)], axis=0)


def _scan_kernel(r_ref, kh_ref, v_ref, kk_ref, ba_ref, lw_ref, g_ref, rk_ref, lnw_ref, lnb_ref,
                 o_ref, st_ref, s_scr):
    c = pl.program_id(1)
    n_pairs = s_scr.shape[0]
    C = CHUNK

    @pl.when(c == 0)
    def _():
        s_scr[...] = jnp.zeros_like(s_scr)

    row = lax.broadcasted_iota(jnp.int32, (C, LANES), 0)
    lane = lax.broadcasted_iota(jnp.int32, (C, LANES), 1)
    src = lane & (HEAD - 1)
    strict = src < row
    incl = src <= row
    eye_w = jnp.where(src == row, 1.0, 0.0)
    lane_lo = lane < HEAD
    sq_r = lax.broadcasted_iota(jnp.int32, (LANES, LANES), 0) >> 6
    sq_c = lax.broadcasted_iota(jnp.int32, (LANES, LANES), 1) >> 6
    same_head = sq_r == sq_c
    ones_bd = jnp.where(same_head, 1.0, 0.0).astype(BF16)
    tri = jnp.where(lax.broadcasted_iota(jnp.int32, (C, C), 1)
                    <= lax.broadcasted_iota(jnp.int32, (C, C), 0), 1.0, 0.0).astype(BF16)

    pairs = range(n_pairs)
    tile = lambda x, p: x[:, p * LANES:(p + 1) * LANES]
    rows_of = lambda x: jnp.concatenate([tile(x, p) for p in pairs], axis=0)
    lanes_of = lambda x: jnp.concatenate([x[p * C:(p + 1) * C] for p in pairs], axis=1)
    pd = lambda x: _pair_diag(x, lane_lo)

    def chunk(sub):
        rs = slice(sub * C, (sub + 1) * C)
        f32 = lambda ref: ref[rs, :].astype(F32)
        r, kh, v = f32(r_ref), f32(kh_ref), f32(v_ref)
        kk, ba, lw = f32(kk_ref), f32(ba_ref), lw_ref[rs, :]
        l_hi, l_mid, l_lo = _split3(lw)
        cs = (jnp.dot(tri, l_hi, preferred_element_type=F32)
              + jnp.dot(tri, l_mid, preferred_element_type=F32)
              + jnp.dot(tri, l_lo, preferred_element_type=F32))
        c_end = cs[C - 1:C, :]
        e_neg = jnp.exp(-cs)
        e_end = jnp.exp(c_end - cs)
        r_t = r * jnp.exp(cs)
        k_x = kk * jnp.exp(cs - lw)
        k_t = kh * e_neg
        b_t = ba * e_neg
        k_e = kh * e_end
        b_e = ba * e_end
        g_end = jnp.exp(c_end)

        gram = [_dot_nt(jnp.concatenate([tile(k_x, p), tile(r_t, p)], axis=0),
                        jnp.concatenate([pd(tile(k_t, p)), pd(tile(b_t, p))], axis=0)) for p in pairs]
        a_kk = [jnp.where(strict, gm[0:C, 0:LANES], 0.0) for gm in gram]
        a_kb = [jnp.where(strict, gm[0:C, LANES:2 * LANES], 0.0) for gm in gram]
        a_rk = [jnp.where(incl, gm[C:2 * C, 0:LANES], 0.0) for gm in gram]
        a_rb = [jnp.where(incl, gm[C:2 * C, LANES:2 * LANES], 0.0) for gm in gram]

        pw = [_dot(a, pd(a)) for a in a_kb]
        t_inv = [eye_w - a for a in a_kb]
        n_lvl = int(math.log2(C)) - 1
        for lvl in range(n_lvl):
            last = lvl == n_lvl - 1
            prod = [_dot(t_inv[p] if last else jnp.concatenate([t_inv[p], pw[p]], axis=0), pd(pw[p]))
                    for p in pairs]
            t_inv = [t_inv[p] + prod[p][0:C] for p in pairs]
            if not last:
                pw = [prod[p][C:2 * C] for p in pairs]

        s_bd = [s_scr[p] for p in pairs]
        v_bd = [pd(tile(v, p)) for p in pairs]
        w_mat = [_dot_nt(tile(k_x, p), s_bd[p]) + _dot(a_kk[p], v_bd[p]) for p in pairs]
        u = [_dot(t_inv[p], pd(w_mat[p])) for p in pairs]
        y = [_dot_nt(tile(r_t, p), s_bd[p]) + _dot(a_rk[p], v_bd[p]) - _dot(a_rb[p], pd(u[p]))
             for p in pairs]
        for p in pairs:
            upd = _dot_tn(jnp.concatenate([tile(v, p), u[p]], axis=0),
                          jnp.concatenate([tile(k_e, p), -tile(b_e, p)], axis=0))
            s_scr[p] = s_bd[p] * tile(g_end, p) + jnp.where(same_head, upd, 0.0)

        y_rows = jnp.concatenate(y, axis=0)
        out = _rwkv_post(y_rows, rows_of(r), rows_of(kh), rows_of(v), rows_of(f32(g_ref)),
                         rk_ref[...], lnw_ref[...], lnb_ref[...], ones_bd, n_pairs)
        o_ref[rs, :] = lanes_of(out).astype(o_ref.dtype)

    for sub in range(r_ref.shape[0] // C):
        chunk(sub)

    @pl.when(c == pl.num_programs(1) - 1)
    def _():
        for p in range(n_pairs):
            s = s_scr[p]
            st_ref[0, 2 * p] = s[0:HEAD, 0:HEAD]
            st_ref[0, 2 * p + 1] = s[HEAD:2 * HEAD, HEAD:2 * HEAD]


def _rwkv_scan(r, kh, v, kk, ba, lw, g, rk, lnw, lnb, nb, t, d):
    rows = CHUNK * SCAN_CHUNKS_PER_STEP
    nc = t // rows
    nh = d // HEAD
    blk = pl.BlockSpec((rows, d), lambda b, c: (b * nc + c, 0))
    par = _const_spec((1, d))
    vmem = 16 * rows * d * 4 + 3 * (d // LANES) * LANES * LANES * 4 + (16 << 20)
    return pl.pallas_call(
        _scan_kernel,
        grid=(nb, nc),
        in_specs=[blk] * 7 + [par] * 3,
        out_specs=[blk, pl.BlockSpec((1, nh, HEAD, HEAD), lambda b, c: (b, 0, 0, 0))],
        out_shape=[jax.ShapeDtypeStruct((nb * t, d), BF16),
                   jax.ShapeDtypeStruct((nb, nh, HEAD, HEAD), F32)],
        scratch_shapes=[pltpu.VMEM((d // LANES, LANES, LANES), F32)],
        compiler_params=_cparams(("arbitrary", "arbitrary"), vmem),
        name="rwkv_scan",
    )(r, kh, v, kk, ba, lw, g, rk, lnw, lnb)


def _step_kernel(s_ref, r_ref, kh_ref, v_ref, kk_ref, ba_ref, lw_ref, g_ref, rk_ref, lnw_ref,
                 lnb_ref, so_ref, o_ref):
    bb, nh = r_ref.shape[0], r_ref.shape[1]
    eye = (lax.broadcasted_iota(jnp.int32, (HEAD, HEAD), 0)
           == lax.broadcasted_iota(jnp.int32, (HEAD, HEAD), 1))
    inv = 1.0 / HEAD

    def body(b, carry):
        r_b, kh_b, v_b = r_ref[b], kh_ref[b], v_ref[b]
        kk_b, ba_b, dec_b = kk_ref[b], ba_ref[b], jnp.exp(lw_ref[b])
        rows = []
        for h in range(nh):
            s = s_ref[b, h]
            sa = -jnp.sum(s * kk_b[h:h + 1], axis=-1, keepdims=True)
            v_col = jnp.sum(jnp.where(eye, v_b[h:h + 1], 0.0), axis=-1, keepdims=True)
            s_new = s * dec_b[h:h + 1] + sa * ba_b[h:h + 1] + v_col * kh_b[h:h + 1]
            so_ref[b, h] = s_new
            y_col = jnp.sum(s_new * r_b[h:h + 1], axis=-1, keepdims=True)
            rows.append(jnp.sum(jnp.where(eye, y_col, 0.0), axis=0, keepdims=True))
        y = jnp.concatenate(rows, axis=0)
        mean = jnp.sum(y, axis=-1, keepdims=True) * inv
        dlt = y - mean
        var = jnp.sum(dlt * dlt, axis=-1, keepdims=True) * inv
        yn = dlt * lax.rsqrt(var + GN_EPS) * lnw_ref[...] + lnb_ref[...]
        bonus = jnp.sum(r_b * kh_b * rk_ref[...], axis=-1, keepdims=True) * v_b
        o_ref[b] = (yn + bonus) * g_ref[b]
        return carry

    lax.fori_loop(0, bb, body, 0)


def _rwkv_step(state, r, kh, v, kk, ba, lw, g, rk, lnw, lnb, bb):
    nb, nh = state.shape[0], state.shape[1]
    vec = pl.BlockSpec((bb, nh, HEAD), lambda i: (i, 0, 0))
    st = pl.BlockSpec((bb, nh, HEAD, HEAD), lambda i: (i, 0, 0, 0))
    par = _const_spec((nh, HEAD))
    vmem = 4 * bb * nh * HEAD * LANES * 4 + (16 << 20)
    return pl.pallas_call(
        _step_kernel,
        grid=(nb // bb,),
        in_specs=[st] + [vec] * 7 + [par] * 3,
        out_specs=[st, vec],
        out_shape=[jax.ShapeDtypeStruct(state.shape, F32),
                   jax.ShapeDtypeStruct((nb, nh, HEAD), F32)],
        compiler_params=_cparams(("arbitrary",), vmem),
        name="rwkv_step",
    )(state, r, kh, v, kk, ba, lw, g, rk, lnw, lnb)


def _alibi_slope(h, n_heads):
    return 2.0 ** (-8.0 * (h + 1) / n_heads)


def _attn_prompt_kernel(sink_ref, q_ref, kp_ref, kc_ref, vp_ref, vc_ref, o_ref):
    n = pl.program_id(1)
    nq = q_ref.shape[1] // HEAD
    group = nq // N_KV
    q = q_ref[...] * SCALE
    k2 = jnp.concatenate([kp_ref[...], kc_ref[...]], axis=0).astype(BF16)
    v2 = jnp.concatenate([vp_ref[...], vc_ref[...]], axis=0).astype(BF16)
    qi = lax.broadcasted_iota(jnp.int32, (WINDOW, 2 * WINDOW), 0)
    si = lax.broadcasted_iota(jnp.int32, (WINDOW, 2 * WINDOW), 1)
    dist = qi + WINDOW - si
    first_key = jnp.where(n > 0, 0, WINDOW)
    valid = (dist >= 0) & (dist < WINDOW) & (si >= first_key)
    dist_m = jnp.where(valid, dist.astype(F32), jnp.inf)
    outs = []
    for h in range(nq):
        j = h // group
        qh = q[:, h * HEAD:(h + 1) * HEAD]
        s = _dot_nt(qh, k2[:, j * HEAD:(j + 1) * HEAD]) - _alibi_slope(h, nq) * dist_m
        sink = sink_ref[h]
        mx = jnp.maximum(jnp.max(s, axis=-1, keepdims=True), sink)
        p = jnp.exp(s - mx)
        den = jnp.sum(p, axis=-1, keepdims=True) + jnp.exp(sink - mx)
        outs.append(_dot(p, v2[:, j * HEAD:(j + 1) * HEAD]) / den)
    o_ref[...] = jnp.concatenate(outs, axis=1).astype(o_ref.dtype)


def _attn_prompt(z, sinks, nb, t, dq, q_blk, k_blk, v_blk):
    nblk = t // WINDOW
    dkv = N_KV * HEAD
    cur = lambda col: (lambda b, n: (b * nblk + n, col))
    prev = lambda col: (lambda b, n: (b * nblk + jnp.maximum(n - 1, 0), col))
    vmem = 4 * WINDOW * (dq * 2 + 4 * dkv) * 4 + (24 << 20)
    return pl.pallas_call(
        _attn_prompt_kernel,
        grid=(nb, nblk),
        in_specs=[pl.BlockSpec(memory_space=pltpu.SMEM),
                  pl.BlockSpec((WINDOW, dq), cur(q_blk)),
                  pl.BlockSpec((WINDOW, dkv), prev(k_blk)),
                  pl.BlockSpec((WINDOW, dkv), cur(k_blk)),
                  pl.BlockSpec((WINDOW, dkv), prev(v_blk)),
                  pl.BlockSpec((WINDOW, dkv), cur(v_blk))],
        out_specs=pl.BlockSpec((WINDOW, dq), lambda b, n: (b * nblk + n, 0)),
        out_shape=jax.ShapeDtypeStruct((nb * t, dq), BF16),
        compiler_params=_cparams(("arbitrary", "arbitrary"), vmem),
        name="attn_prompt",
    )(sinks, z, z, z, z, z)


def _attn_sample_kernel(q_ref, kn_ref, vn_ref, ck_ref, cv_ref, sink_ref, slope_ref,
                        o_ref, ko_ref, vo_ref):
    bb, nq = q_ref.shape[0], q_ref.shape[1]
    group = nq // N_KV
    dkv = N_KV * HEAD
    hrow = lax.broadcasted_iota(jnp.int32, (nq, dkv), 0) // group
    hlane = lax.broadcasted_iota(jnp.int32, (nq, dkv), 1) >> 6
    own = hrow == hlane
    si = lax.broadcasted_iota(jnp.int32, (nq, WINDOW), 1)
    distf = (WINDOW - si).astype(F32)
    valid = si >= 1
    sink = sink_ref[...]
    slope = slope_ref[...]

    def body(b, carry):
        q = q_ref[b]
        q_bd = jnp.where(own, jnp.concatenate([q] * N_KV, axis=1), 0.0)
        ck = ck_ref[b]
        cv = cv_ref[b]
        kn = kn_ref[b]
        vn = vn_ref[b]
        s_c = _dot_nt(q_bd, ck) * SCALE - slope * distf
        s_c = jnp.where(valid, s_c, -jnp.inf)
        s_n = jnp.sum(q_bd * kn, axis=-1, keepdims=True) * SCALE
        mx = jnp.maximum(jnp.maximum(jnp.max(s_c, axis=-1, keepdims=True), s_n), sink)
        p_c = jnp.exp(s_c - mx)
        p_n = jnp.exp(s_n - mx)
        den = jnp.sum(p_c, axis=-1, keepdims=True) + p_n + jnp.exp(sink - mx)
        o_all = (_dot(p_c, cv) + p_n * vn) / den
        o_ref[b] = jnp.concatenate(
            [o_all[j * group:(j + 1) * group, j * HEAD:(j + 1) * HEAD] for j in range(N_KV)], axis=0)
        ko_ref[b] = jnp.concatenate([ck[1:], kn], axis=0)
        vo_ref[b] = jnp.concatenate([cv[1:], vn], axis=0)
        return carry

    lax.fori_loop(0, bb, body, 0)


def _attn_sample(q, kn, vn, ck, cv, sinks, slopes, bb):
    nb, nq = q.shape[0], q.shape[1]
    dkv = N_KV * HEAD
    cache = pl.BlockSpec((bb, WINDOW, dkv), lambda i: (i, 0, 0))
    new = pl.BlockSpec((bb, 1, dkv), lambda i: (i, 0, 0))
    qs = pl.BlockSpec((bb, nq, HEAD), lambda i: (i, 0, 0))
    vmem = 8 * bb * WINDOW * dkv * 4 + (16 << 20)
    return pl.pallas_call(
        _attn_sample_kernel,
        grid=(nb // bb,),
        in_specs=[qs, new, new, cache, cache, _const_spec((nq, 1)), _const_spec((nq, 1))],
        out_specs=[qs, cache, cache],
        out_shape=[jax.ShapeDtypeStruct((nb, nq, HEAD), F32),
                   jax.ShapeDtypeStruct(ck.shape, F32),
                   jax.ShapeDtypeStruct(cv.shape, F32)],
        compiler_params=_cparams(("arbitrary",), vmem),
        name="attn_sample",
    )(q, kn, vn, ck, cv, sinks, slopes)


def _out_proj_kernel(a_ref, b_ref, x_ref, w_ref, o_ref):
    d = a_ref.shape[1]
    o_ref[...] = (x_ref[...]
                  + jnp.dot(a_ref[...].astype(BF16), w_ref[0:d, :], preferred_element_type=F32)
                  + jnp.dot(b_ref[...].astype(BF16), w_ref[d:, :], preferred_element_type=F32))


def _out_proj(a, b, x, w, tm):
    m, d = a.shape
    dm = x.shape[1]
    vmem = 4 * tm * d * 4 + 4 * tm * dm * 4 + 2 * 2 * d * dm * 2 + (8 << 20)
    return pl.pallas_call(
        _out_proj_kernel,
        grid=(m // tm,),
        in_specs=[pl.BlockSpec((tm, d), lambda i: (i, 0)),
                  pl.BlockSpec((tm, d), lambda i: (i, 0)),
                  pl.BlockSpec((tm, dm), lambda i: (i, 0)),
                  _const_spec((2 * d, dm))],
        out_specs=pl.BlockSpec((tm, dm), lambda i: (i, 0)),
        out_shape=jax.ShapeDtypeStruct((m, dm), F32),
        compiler_params=_cparams(("arbitrary",), vmem),
        name="out_proj",
    )(a, b, x, w)


def _ffn_kernel(x_ref, g_ref, wg_ref, wu_ref, wd_ref, o_ref, h_ref, acc_ref):
    j = pl.program_id(1)

    @pl.when(j == 0)
    def _():
        h_ref[...] = _rms(x_ref[...], g_ref[...]).astype(BF16)
        acc_ref[...] = jnp.zeros_like(acc_ref)

    h = h_ref[...]
    gate = jnp.dot(h, wg_ref[...], preferred_element_type=F32)
    up = jnp.dot(h, wu_ref[...], preferred_element_type=F32)
    act = gate * _sigmoid(gate) * up
    acc_ref[...] += jnp.dot(act.astype(BF16), wd_ref[...], preferred_element_type=F32)

    @pl.when(j == pl.num_programs(1) - 1)
    def _():
        o_ref[...] = x_ref[...] + acc_ref[...]


def _ffn(x, gain, wg, wu, wd, tm, tf):
    m, dm = x.shape
    f = wg.shape[1]
    vmem = 4 * tm * dm * 4 + tm * dm * 2 + tm * dm * 4 + 6 * dm * tf * 2 + 4 * tm * tf * 4 + (6 << 20)
    return pl.pallas_call(
        _ffn_kernel,
        grid=(m // tm, f // tf),
        in_specs=[pl.BlockSpec((tm, dm), lambda i, j: (i, 0)),
                  _const_spec((1, dm)),
                  pl.BlockSpec((dm, tf), lambda i, j: (0, j)),
                  pl.BlockSpec((dm, tf), lambda i, j: (0, j)),
                  pl.BlockSpec((tf, dm), lambda i, j: (j, 0))],
        out_specs=pl.BlockSpec((tm, dm), lambda i, j: (i, 0)),
        out_shape=jax.ShapeDtypeStruct((m, dm), F32),
        scratch_shapes=[pltpu.VMEM((tm, dm), BF16), pltpu.VMEM((tm, dm), F32)],
        compiler_params=_cparams(("arbitrary", "arbitrary"), vmem),
        name="ffn",
    )(x, gain, wg, wu, wd)


def _ple_kernel(x_ref, p_ref, gn_ref, wgate_ref, wproj_ref, fn_ref, o_ref):
    x = x_ref[...]
    gate = _sigmoid(jnp.dot(_rms(x, gn_ref[...]).astype(BF16), wgate_ref[...],
                            preferred_element_type=F32))
    emb = jnp.dot(p_ref[...].astype(BF16), wproj_ref[...], preferred_element_type=F32)
    o_ref[...] = _rms(x + gate * emb, fn_ref[...])


def _ple_final(x, p, gn, wgate, wproj, fn, tm):
    m, dm = x.shape
    dp = p.shape[1]
    vmem = 4 * tm * dm * 4 + 2 * tm * dp * 4 + 2 * dm * dm * 2 + 2 * dp * dm * 2 + 4 * tm * dm * 4 + (6 << 20)
    return pl.pallas_call(
        _ple_kernel,
        grid=(m // tm,),
        in_specs=[pl.BlockSpec((tm, dm), lambda i: (i, 0)),
                  pl.BlockSpec((tm, dp), lambda i: (i, 0)),
                  _const_spec((1, dm)), _const_spec((dm, dm)), _const_spec((dp, dm)),
                  _const_spec((1, dm))],
        out_specs=pl.BlockSpec((tm, dm), lambda i: (i, 0)),
        out_shape=jax.ShapeDtypeStruct((m, dm), F32),
        compiler_params=_cparams(("arbitrary",), vmem),
        name="ple_final",
    )(x, p, gn, wgate, wproj, fn)


def _pad_cols(a, n):
    return jnp.pad(a, ((0, 0), (0, n - a.shape[1])))


def _layer(x, p_emb, lw_, final_norm, *, nb, t, state, shift_prev, caches):
    m, dm = x.shape
    d = lw_["d"]
    dq = lw_["dq"]
    dkv = N_KV * HEAD
    sample = caches is not None
    o_q, o_k, o_v, o_l = 3 * d, 3 * d + dq, 3 * d + dq + dkv, 3 * d + dq + 2 * dkv
    tm = min(m, 1024)
    z = _norm_matmul(x, lw_["norm_mix"], lw_["w_in"], tm, 1280)
    lora_blk = o_l // 512
    nh = d // HEAD
    if not sample:
        feats = _rwkv_prep_prompt(z, lw_["prep"], nb, t, d, 256, lora_blk)
        out_r, wkv = _rwkv_scan(*feats, lw_["r_k"], lw_["ln_w"], lw_["ln_b"], nb, t, d)
        y_a = _attn_prompt(z, lw_["sinks"], nb, t, dq, o_q // dq, o_k // dkv, o_v // dkv)
        z3 = z.reshape(nb, t, -1)
        shift_rows = z3[:, -1].astype(F32)
        k_new = z3[:, -WINDOW:, o_k:o_k + dkv].astype(F32).reshape(nb, WINDOW, N_KV, HEAD)
        v_new = z3[:, -WINDOW:, o_v:o_v + dkv].astype(F32).reshape(nb, WINDOW, N_KV, HEAD)
    else:
        prev_r = shift_prev[:, :3 * d]
        prev_l = jnp.concatenate(
            [_pad_cols(shift_prev[:, 3 * d:3 * d + LORA_W], LORA_PAD),
             _pad_cols(shift_prev[:, 3 * d + LORA_W:3 * d + LORA_W + LORA_A], LORA_PAD),
             shift_prev[:, 3 * d + LORA_W + LORA_A:]], axis=1)
        feats = _rwkv_prep_sample(z, prev_r, prev_l, lw_["prep"], d, lora_blk)
        feats3 = [f.reshape(m, nh, HEAD) for f in feats]
        wkv, out_r3 = _rwkv_step(state, *feats3, lw_["r_k"].reshape(nh, HEAD),
                                 lw_["ln_w"].reshape(nh, HEAD), lw_["ln_b"].reshape(nh, HEAD), 8)
        out_r = out_r3.reshape(m, d)
        ck, cv = caches
        q3 = z[:, o_q:o_q + dq].astype(F32).reshape(m, dq // HEAD, HEAD)
        kn = z[:, o_k:o_k + dkv].astype(F32).reshape(m, 1, dkv)
        vn = z[:, o_v:o_v + dkv].astype(F32).reshape(m, 1, dkv)
        y3, k_new, v_new = _attn_sample(q3, kn, vn, ck.reshape(m, WINDOW, dkv),
                                        cv.reshape(m, WINDOW, dkv), lw_["sinks"].reshape(-1, 1),
                                        lw_["slopes"], 8)
        y_a = y3.reshape(m, dq)
        k_new = k_new.reshape(m, WINDOW, N_KV, HEAD)
        v_new = v_new.reshape(m, WINDOW, N_KV, HEAD)
        shift_rows = z.astype(F32)
    shift_new = jnp.concatenate(
        [shift_rows[:, :3 * d], shift_rows[:, o_l:o_l + LORA_W],
         shift_rows[:, o_l + LORA_PAD:o_l + LORA_PAD + LORA_A],
         shift_rows[:, o_l + 2 * LORA_PAD:o_l + 2 * LORA_PAD + LORA_G]], axis=1)
    tm2 = min(m, 512)
    x1 = _out_proj(out_r, y_a, x, lw_["w_out"], tm2)
    x2 = _ffn(x1, lw_["norm_ffn"], lw_["w_gate"], lw_["w_up"], lw_["w_down"], tm2, 512)
    y = _ple_final(x2, p_emb, lw_["norm_ple"], lw_["ple_gate"], lw_["ple_proj"], final_norm, tm2)
    return y, wkv, shift_new, k_new, v_new


def kernel(x_prompt, x_sample, state_wkv, state_shift, cache_k, cache_v, p_prompt, p_sample, norm_mix, w_in, mu_shift, rwkv_w0, rwkv_w2, rwkv_a0, rwkv_a2, rwkv_g2, rwkv_k_k, rwkv_k_a, rwkv_r_k, rwkv_ln_w, rwkv_ln_b, attn_sinks, w_out, norm_ffn, w_gate, w_up, w_down, norm_ple, ple_gate, ple_proj, final_norm):
    depth = w_in.shape[0]
    assert depth == 1, "single-layer step"
    bp, t, dm = x_prompt.shape
    bs = x_sample.shape[0]
    assert x_sample.shape[1] == 1
    d = rwkv_w0.shape[1]
    dq = attn_sinks.shape[1] * HEAD
    dkv = N_KV * HEAD
    i = 0
    row = lambda a: a.reshape(1, -1)
    wi = w_in[i]
    rp = 3 * d + LORA_W + LORA_A + LORA_G
    w_in_p = jnp.concatenate(
        [wi[:, :3 * d], wi[:, rp:],
         _pad_cols(wi[:, 3 * d:3 * d + LORA_W], LORA_PAD),
         _pad_cols(wi[:, 3 * d + LORA_W:3 * d + LORA_W + LORA_A], LORA_PAD),
         wi[:, 3 * d + LORA_W + LORA_A:rp]], axis=1).astype(BF16)
    mu = row(mu_shift[i])
    mu_r = mu[:, :3 * d]
    mu_l = jnp.concatenate(
        [_pad_cols(mu[:, 3 * d:3 * d + LORA_W], LORA_PAD),
         _pad_cols(mu[:, 3 * d + LORA_W:3 * d + LORA_W + LORA_A], LORA_PAD),
         mu[:, 3 * d + LORA_W + LORA_A:]], axis=1)
    pad_rows = lambda a: jnp.pad(a, ((0, LORA_PAD - a.shape[0]), (0, 0))).astype(BF16)
    nq = dq // HEAD
    lw_ = {
        "d": d, "dq": dq,
        "norm_mix": row(norm_mix[i]), "w_in": w_in_p,
        "prep": (mu_r, mu_l, row(rwkv_w0[i]), pad_rows(rwkv_w2[i]), row(rwkv_a0[i]),
                 pad_rows(rwkv_a2[i]), rwkv_g2[i].astype(BF16), row(rwkv_k_k[i]), row(rwkv_k_a[i])),
        "r_k": row(rwkv_r_k[i]), "ln_w": row(rwkv_ln_w[i]), "ln_b": row(rwkv_ln_b[i]),
        "sinks": attn_sinks[i],
        "slopes": (2.0 ** (-8.0 * jnp.arange(1, nq + 1, dtype=F32) / nq)).reshape(nq, 1),
        "w_out": w_out[i].astype(BF16), "norm_ffn": row(norm_ffn[i]),
        "w_gate": w_gate[i].astype(BF16), "w_up": w_up[i].astype(BF16),
        "w_down": w_down[i].astype(BF16), "norm_ple": row(norm_ple[i]),
        "ple_gate": ple_gate[i].astype(BF16), "ple_proj": ple_proj[i].astype(BF16),
    }
    fn = row(final_norm)
    yp, wkv_p, shift_p, k_p, v_p = _layer(
        x_prompt.reshape(bp * t, dm), p_prompt[i].reshape(bp * t, -1), lw_, fn,
        nb=bp, t=t, state=None, shift_prev=None, caches=None)
    ys, wkv_s, shift_s, k_s, v_s = _layer(
        x_sample.reshape(bs, dm), p_sample[i].reshape(bs, -1), lw_, fn,
        nb=bs, t=1, state=state_wkv[i], shift_prev=state_shift[i],
        caches=(cache_k[i], cache_v[i]))
    return (yp.reshape(bp, t, dm), ys.reshape(bs, 1, dm),
            wkv_p[None], shift_p[None], k_p[None], v_p[None],
            wkv_s[None], shift_s[None], k_s[None], v_s[None])
```

```python
import functools
import math

import jax
import jax.numpy as jnp
from jax import lax
from jax.experimental import pallas as pl
from jax.experimental.pallas import tpu as pltpu

F32 = jnp.float32
BF16 = jnp.bfloat16

HEAD = 64
LANES = 128
N_KV = 4
WINDOW = 128
LORA_W = 96
LORA_A = 96
LORA_G = 256
LORA_PAD = 128
EPS = 1e-6
GN_EPS = 64e-5
SCALE = HEAD ** -0.5
assert math.log2(SCALE).is_integer()
CHUNK = 64
SCAN_CHUNKS_PER_STEP = 4
DECAY_SCALE = math.exp(-0.5)
VMEM_CAP = 60 * 1024 * 1024


def _cparams(sem, vmem_bytes):
    return pltpu.CompilerParams(dimension_semantics=sem,
                                vmem_limit_bytes=int(min(VMEM_CAP, vmem_bytes)))


def _dot(a, b):
    return jnp.dot(a.astype(BF16), b.astype(BF16), preferred_element_type=F32)


def _dot_nt(a, b):
    return lax.dot_general(a.astype(BF16), b.astype(BF16), (((1,), (1,)), ((), ())),
                           preferred_element_type=F32)


def _dot_tn(a, b):
    return lax.dot_general(a.astype(BF16), b.astype(BF16), (((0,), (0,)), ((), ())),
                           preferred_element_type=F32)


def _split2(x):
    hi = x.astype(BF16)
    lo = (x - hi.astype(F32)).astype(BF16)
    return hi, lo


def _split3(x):
    hi = x.astype(BF16)
    r1 = x - hi.astype(F32)
    mid = r1.astype(BF16)
    lo = (r1 - mid.astype(F32)).astype(BF16)
    return hi, mid, lo


def _head_ones():
    i = lax.broadcasted_iota(jnp.int32, (LANES, LANES), 0) >> 6
    j = lax.broadcasted_iota(jnp.int32, (LANES, LANES), 1) >> 6
    return jnp.where(i == j, 1.0, 0.0).astype(BF16)


def _segsum(x, ones_bd):
    hi, lo = _split2(x)
    return (jnp.dot(hi, ones_bd, preferred_element_type=F32)
            + jnp.dot(lo, ones_bd, preferred_element_type=F32))


def _rms(x, g):
    ms = jnp.mean(x * x, axis=-1, keepdims=True)
    return x * lax.rsqrt(ms + EPS) * g


def _sigmoid(x):
    return 1.0 / (1.0 + jnp.exp(-x))


def _norm_mm_kernel(x_ref, g_ref, w_ref, o_ref, h_ref):
    @pl.when(pl.program_id(1) == 0)
    def _():
        h_ref[...] = _rms(x_ref[...], g_ref[...]).astype(BF16)

    o_ref[...] = jnp.dot(h_ref[...], w_ref[...], preferred_element_type=F32).astype(o_ref.dtype)


def _norm_matmul(x, gain, w, tm, tn):
    m, k = x.shape
    n = w.shape[1]
    vmem = 2 * tm * k * 4 + 2 * k * tn * 2 + 2 * tm * tn * 4 + tm * k * 2 + (4 << 20)
    return pl.pallas_call(
        _norm_mm_kernel,
        grid=(m // tm, n // tn),
        in_specs=[pl.BlockSpec((tm, k), lambda i, j: (i, 0)),
                  pl.BlockSpec((1, k), lambda i, j: (0, 0)),
                  pl.BlockSpec((k, tn), lambda i, j: (0, j))],
        out_specs=pl.BlockSpec((tm, tn), lambda i, j: (i, j)),
        out_shape=jax.ShapeDtypeStruct((m, n), BF16),
        scratch_shapes=[pltpu.VMEM((tm, k), BF16)],
        compiler_params=_cparams(("arbitrary", "arbitrary"), vmem),
        name="norm_in_proj",
    )(x, gain, w)


def _prep_core(zr, zpr, zl, zpl, mur_ref, mul_ref, w0_ref, w2_ref, a0_ref, a2_ref, g2_ref,
               kk_ref, ka_ref, out_refs):
    r_o, kh_o, v_o, kk_o, ba_o, lw_o, g_o = out_refs
    d = r_o.shape[-1]
    xr = zr + (zpr - zr) * mur_ref[...]
    xl = zl + (zpl - zl) * mul_ref[...]
    r = xr[:, 0:d]
    k = xr[:, d:2 * d]
    v = xr[:, 2 * d:3 * d]
    xw = xl[:, 0:LORA_PAD]
    xa = xl[:, LORA_PAD:2 * LORA_PAD]
    xg = xl[:, 2 * LORA_PAD:2 * LORA_PAD + LORA_G]
    u = w0_ref[...] + jnp.dot(jnp.tanh(xw).astype(BF16), w2_ref[...], preferred_element_type=F32)
    lw = -DECAY_SCALE * _sigmoid(u)
    a = _sigmoid(a0_ref[...] + jnp.dot(xa.astype(BF16), a2_ref[...], preferred_element_type=F32))
    g = jnp.dot(_sigmoid(xg).astype(BF16), g2_ref[...], preferred_element_type=F32)
    kk = k * kk_ref[...]
    ones_bd = _head_ones()
    kk2 = kk * kk
    ss = jnp.concatenate([_segsum(kk2[:, t * LANES:(t + 1) * LANES], ones_bd)
                          for t in range(d // LANES)], axis=1)
    kk = kk * lax.rsqrt(jnp.maximum(ss, 1e-24))
    kh = k * (1.0 + (a - 1.0) * ka_ref[...])
    r_o[...] = r.astype(r_o.dtype)
    kh_o[...] = kh.astype(kh_o.dtype)
    v_o[...] = v.astype(v_o.dtype)
    kk_o[...] = kk.astype(kk_o.dtype)
    ba_o[...] = (kk * a).astype(ba_o.dtype)
    lw_o[...] = lw
    g_o[...] = g.astype(g_o.dtype)


def _prep_prompt_kernel(zr_ref, zl_ref, mur, mul, w0, w2, a0, a2, g2, kkp, kap,
                        r_o, kh_o, v_o, kk_o, ba_o, lw_o, g_o, cr_ref, cl_ref):
    @pl.when(pl.program_id(1) == 0)
    def _():
        cr_ref[...] = jnp.zeros_like(cr_ref)
        cl_ref[...] = jnp.zeros_like(cl_ref)

    zr = zr_ref[...].astype(F32)
    zl = zl_ref[...].astype(F32)
    tt = zr.shape[0]
    first = lax.broadcasted_iota(jnp.int32, (tt, 1), 0) == 0
    zpr = jnp.where(first, cr_ref[0:1, :], pltpu.roll(zr, 1, 0))
    zpl = jnp.where(first, cl_ref[0:1, :], pltpu.roll(zl, 1, 0))
    cr_ref[0:1, :] = zr[tt - 1:tt, :]
    cl_ref[0:1, :] = zl[tt - 1:tt, :]
    _prep_core(zr, zpr, zl, zpl, mur, mul, w0, w2, a0, a2, g2, kkp, kap,
               (r_o, kh_o, v_o, kk_o, ba_o, lw_o, g_o))


def _prep_sample_kernel(zr_ref, zl_ref, pr_ref, pl_ref, mur, mul, w0, w2, a0, a2, g2, kkp, kap,
                        r_o, kh_o, v_o, kk_o, ba_o, lw_o, g_o):
    _prep_core(zr_ref[...].astype(F32), pr_ref[...], zl_ref[...].astype(F32), pl_ref[...], mur, mul, w0, w2, a0, a2, g2,
               kkp, kap, (r_o, kh_o, v_o, kk_o, ba_o, lw_o, g_o))


def _const_spec(shape):
    return pl.BlockSpec(shape, lambda *_: (0,) * len(shape))


def _prep_param_specs(d):
    return [_const_spec((1, 3 * d)), _const_spec((1, 512)), _const_spec((1, d)),
            _const_spec((LORA_PAD, d)), _const_spec((1, d)), _const_spec((LORA_PAD, d)),
            _const_spec((LORA_G, d)), _const_spec((1, d)), _const_spec((1, d))]


def _rwkv_prep_prompt(z, params, nb, t, d, tt, lora_blk):
    m = nb * t
    nt = t // tt
    row = lambda b, i: (b * nt + i, 0)
    act = jax.ShapeDtypeStruct((m, d), BF16)
    out_shape = [act] * 5 + [jax.ShapeDtypeStruct((m, d), F32), act]
    vmem = 2 * tt * (3 * d + 512) * 4 + 14 * tt * d * 4 + 24 * tt * d * 4 + (8 << 20)
    return pl.pallas_call(
        _prep_prompt_kernel,
        grid=(nb, nt),
        in_specs=[pl.BlockSpec((tt, 3 * d), row),
                  pl.BlockSpec((tt, 512), lambda b, i: (b * nt + i, lora_blk))]
                 + _prep_param_specs(d),
        out_specs=[pl.BlockSpec((tt, d), row)] * 7,
        out_shape=out_shape,
        scratch_shapes=[pltpu.VMEM((8, 3 * d), F32), pltpu.VMEM((8, 512), F32)],
        compiler_params=_cparams(("arbitrary", "arbitrary"), vmem),
        name="rwkv_prep_prompt",
    )(z, z, *params)


def _rwkv_prep_sample(z, prev_r, prev_l, params, d, lora_blk):
    m = z.shape[0]
    out = jax.ShapeDtypeStruct((m, d), F32)
    vmem = 4 * m * (3 * d + 512) * 4 + 14 * m * d * 4 + 24 * m * d * 4 + (8 << 20)
    return pl.pallas_call(
        _prep_sample_kernel,
        grid=(1,),
        in_specs=[pl.BlockSpec((m, 3 * d), lambda i: (0, 0)),
                  pl.BlockSpec((m, 512), lambda i: (0, lora_blk)),
                  _const_spec((m, 3 * d)), _const_spec((m, 512))]
                 + _prep_param_specs(d),
        out_specs=[pl.BlockSpec((m, d), lambda i: (0, 0))] * 7,
        out_shape=[out] * 7,
        compiler_params=_cparams(("arbitrary",), vmem),
        name="rwkv_prep_sample",
    )(z, z, prev_r, prev_l, *params)


def _rwkv_post(y, r, kh, v, g, rk, lnw, lnb, ones_bd, n_tiles):
    rows = y.shape[0] // n_tiles
    par = lambda a: jnp.concatenate(
        [jnp.broadcast_to(a[:, p * LANES:(p + 1) * LANES], (rows, LANES)) for p in range(n_tiles)], axis=0)
    inv = 1.0 / HEAD
    mean = _segsum(y, ones_bd) * inv
    dlt = y - mean
    var = _segsum(dlt * dlt, ones_bd) * inv
    yn = dlt * lax.rsqrt(var + GN_EPS) * par(lnw) + par(lnb)
    bonus = _segsum(r * kh * par(rk), ones_bd) * v
    return (yn + bonus) * g


def _pair_diag(x, lane_lo):
    return jnp.concatenate([jnp.where(lane_lo, x, 0.0), jnp.where(lane_lo, 0.0, x)], axis=0)


def _scan_kernel(r_ref, kh_ref, v_ref, kk_ref, ba_ref, lw_ref, g_ref, rk_ref, lnw_ref, lnb_ref,
                 o_ref, st_ref, s_scr):
    c = pl.program_id(1)
    n_pairs = s_scr.shape[0]
    C = CHUNK

    @pl.when(c == 0)
    def _():
        s_scr[...] = jnp.zeros_like(s_scr)

    row = lax.broadcasted_iota(jnp.int32, (C, LANES), 0)
    lane = lax.broadcasted_iota(jnp.int32, (C, LANES), 1)
    src = lane & (HEAD - 1)
    strict = src < row
    incl = src <= row
    eye_w = jnp.where(src == row, 1.0, 0.0)
    lane_lo = lane < HEAD
    sq_r = lax.broadcasted_iota(jnp.int32, (LANES, LANES), 0) >> 6
    sq_c = lax.broadcasted_iota(jnp.int32, (LANES, LANES), 1) >> 6
    same_head = sq_r == sq_c
    ones_bd = jnp.where(same_head, 1.0, 0.0).astype(BF16)
    tri = jnp.where(lax.broadcasted_iota(jnp.int32, (C, C), 1)
                    <= lax.broadcasted_iota(jnp.int32, (C, C), 0), 1.0, 0.0).astype(BF16)

    pairs = range(n_pairs)
    tile = lambda x, p: x[:, p * LANES:(p + 1) * LANES]
    rows_of = lambda x: jnp.concatenate([tile(x, p) for p in pairs], axis=0)
    lanes_of = lambda x: jnp.concatenate([x[p * C:(p + 1) * C] for p in pairs], axis=1)
    pd = lambda x: _pair_diag(x, lane_lo)

    def chunk(sub):
        rs = slice(sub * C, (sub + 1) * C)
        f32 = lambda ref: ref[rs, :].astype(F32)
        r, kh, v = f32(r_ref), f32(kh_ref), f32(v_ref)
        kk, ba, lw = f32(kk_ref), f32(ba_ref), lw_ref[rs, :]
        l_hi, l_mid, l_lo = _split3(lw)
        cs = (jnp.dot(tri, l_hi, preferred_element_type=F32)
              + jnp.dot(tri, l_mid, preferred_element_type=F32)
              + jnp.dot(tri, l_lo, preferred_element_type=F32))
        c_end = cs[C - 1:C, :]
        e_neg = jnp.exp(-cs)
        e_end = jnp.exp(c_end - cs)
        r_t = r * jnp.exp(cs)
        k_x = kk * jnp.exp(cs - lw)
        k_t = kh * e_neg
        b_t = ba * e_neg
        k_e = kh * e_end
        b_e = ba * e_end
        g_end = jnp.exp(c_end)

        gram = [_dot_nt(jnp.concatenate([tile(k_x, p), tile(r_t, p)], axis=0),
                        jnp.concatenate([pd(tile(k_t, p)), pd(tile(b_t, p))], axis=0)) for p in pairs]
        a_kk = [jnp.where(strict, gm[0:C, 0:LANES], 0.0) for gm in gram]
        a_kb = [jnp.where(strict, gm[0:C, LANES:2 * LANES], 0.0) for gm in gram]
        a_rk = [jnp.where(incl, gm[C:2 * C, 0:LANES], 0.0) for gm in gram]
        a_rb = [jnp.where(incl, gm[C:2 * C, LANES:2 * LANES], 0.0) for gm in gram]

        pw = [_dot(a, pd(a)) for a in a_kb]
        t_inv = [eye_w - a for a in a_kb]
        n_lvl = int(math.log2(C)) - 1
        for lvl in range(n_lvl):
            last = lvl == n_lvl - 1
            prod = [_dot(t_inv[p] if last else jnp.concatenate([t_inv[p], pw[p]], axis=0), pd(pw[p]))
                    for p in pairs]
            t_inv = [t_inv[p] + prod[p][0:C] for p in pairs]
            if not last:
                pw = [prod[p][C:2 * C] for p in pairs]

        s_bd = [s_scr[p] for p in pairs]
        v_bd = [pd(tile(v, p)) for p in pairs]
        w_mat = [_dot_nt(tile(k_x, p), s_bd[p]) + _dot(a_kk[p], v_bd[p]) for p in pairs]
        u = [_dot(t_inv[p], pd(w_mat[p])) for p in pairs]
        y = [_dot_nt(tile(r_t, p), s_bd[p]) + _dot(a_rk[p], v_bd[p]) - _dot(a_rb[p], pd(u[p]))
             for p in pairs]
        for p in pairs:
            upd = _dot_tn(jnp.concatenate([tile(v, p), u[p]], axis=0),
                          jnp.concatenate([tile(k_e, p), -tile(b_e, p)], axis=0))
            s_scr[p] = s_bd[p] * tile(g_end, p) + jnp.where(same_head, upd, 0.0)

        y_rows = jnp.concatenate(y, axis=0)
        out = _rwkv_post(y_rows, rows_of(r), rows_of(kh), rows_of(v), rows_of(f32(g_ref)),
                         rk_ref[...], lnw_ref[...], lnb_ref[...], ones_bd, n_pairs)
        o_ref[rs, :] = lanes_of(out).astype(o_ref.dtype)

    for sub in range(r_ref.shape[0] // C):
        chunk(sub)

    @pl.when(c == pl.num_programs(1) - 1)
    def _():
        for p in range(n_pairs):
            s = s_scr[p]
            st_ref[0, 2 * p] = s[0:HEAD, 0:HEAD]
            st_ref[0, 2 * p + 1] = s[HEAD:2 * HEAD, HEAD:2 * HEAD]


def _rwkv_scan(r, kh, v, kk, ba, lw, g, rk, lnw, lnb, nb, t, d):
    rows = CHUNK * SCAN_CHUNKS_PER_STEP
    nc = t // rows
    nh = d // HEAD
    blk = pl.BlockSpec((rows, d), lambda b, c: (b * nc + c, 0))
    par = _const_spec((1, d))
    vmem = 16 * rows * d * 4 + 3 * (d // LANES) * LANES * LANES * 4 + (16 << 20)
    return pl.pallas_call(
        _scan_kernel,
        grid=(nb, nc),
        in_specs=[blk] * 7 + [par] * 3,
        out_specs=[blk, pl.BlockSpec((1, nh, HEAD, HEAD), lambda b, c: (b, 0, 0, 0))],
        out_shape=[jax.ShapeDtypeStruct((nb * t, d), BF16),
                   jax.ShapeDtypeStruct((nb, nh, HEAD, HEAD), F32)],
        scratch_shapes=[pltpu.VMEM((d // LANES, LANES, LANES), F32)],
        compiler_params=_cparams(("arbitrary", "arbitrary"), vmem),
        name="rwkv_scan",
    )(r, kh, v, kk, ba, lw, g, rk, lnw, lnb)


def _step_kernel(s_ref, r_ref, kh_ref, v_ref, kk_ref, ba_ref, lw_ref, g_ref, rk_ref, lnw_ref,
                 lnb_ref, so_ref, o_ref):
    bb, nh = r_ref.shape[0], r_ref.shape[1]
    eye = (lax.broadcasted_iota(jnp.int32, (HEAD, HEAD), 0)
           == lax.broadcasted_iota(jnp.int32, (HEAD, HEAD), 1))
    inv = 1.0 / HEAD

    def body(b, carry):
        r_b, kh_b, v_b = r_ref[b], kh_ref[b], v_ref[b]
        kk_b, ba_b, dec_b = kk_ref[b], ba_ref[b], jnp.exp(lw_ref[b])
        rows = []
        for h in range(nh):
            s = s_ref[b, h]
            sa = -jnp.sum(s * kk_b[h:h + 1], axis=-1, keepdims=True)
            v_col = jnp.sum(jnp.where(eye, v_b[h:h + 1], 0.0), axis=-1, keepdims=True)
            s_new = s * dec_b[h:h + 1] + sa * ba_b[h:h + 1] + v_col * kh_b[h:h + 1]
            so_ref[b, h] = s_new
            y_col = jnp.sum(s_new * r_b[h:h + 1], axis=-1, keepdims=True)
            rows.append(jnp.sum(jnp.where(eye, y_col, 0.0), axis=0, keepdims=True))
        y = jnp.concatenate(rows, axis=0)
        mean = jnp.sum(y, axis=-1, keepdims=True) * inv
        dlt = y - mean
        var = jnp.sum(dlt * dlt, axis=-1, keepdims=True) * inv
        yn = dlt * lax.rsqrt(var + GN_EPS) * lnw_ref[...] + lnb_ref[...]
        bonus = jnp.sum(r_b * kh_b * rk_ref[...], axis=-1, keepdims=True) * v_b
        o_ref[b] = (yn + bonus) * g_ref[b]
        return carry

    lax.fori_loop(0, bb, body, 0)


def _rwkv_step(state, r, kh, v, kk, ba, lw, g, rk, lnw, lnb, bb):
    nb, nh = state.shape[0], state.shape[1]
    vec = pl.BlockSpec((bb, nh, HEAD), lambda i: (i, 0, 0))
    st = pl.BlockSpec((bb, nh, HEAD, HEAD), lambda i: (i, 0, 0, 0))
    par = _const_spec((nh, HEAD))
    vmem = 4 * bb * nh * HEAD * LANES * 4 + (16 << 20)
    return pl.pallas_call(
        _step_kernel,
        grid=(nb // bb,),
        in_specs=[st] + [vec] * 7 + [par] * 3,
        out_specs=[st, vec],
        out_shape=[jax.ShapeDtypeStruct(state.shape, F32),
                   jax.ShapeDtypeStruct((nb, nh, HEAD), F32)],
        compiler_params=_cparams(("arbitrary",), vmem),
        name="rwkv_step",
    )(state, r, kh, v, kk, ba, lw, g, rk, lnw, lnb)


def _alibi_slope(h, n_heads):
    return 2.0 ** (-8.0 * (h + 1) / n_heads)


def _attn_prompt_kernel(sink_ref, q_ref, kp_ref, kc_ref, vp_ref, vc_ref, o_ref):
    n = pl.program_id(1)
    nq = q_ref.shape[1] // HEAD
    group = nq // N_KV
    q = q_ref[...] * SCALE
    k2 = jnp.concatenate([kp_ref[...], kc_ref[...]], axis=0).astype(BF16)
    v2 = jnp.concatenate([vp_ref[...], vc_ref[...]], axis=0).astype(BF16)
    qi = lax.broadcasted_iota(jnp.int32, (WINDOW, 2 * WINDOW), 0)
    si = lax.broadcasted_iota(jnp.int32, (WINDOW, 2 * WINDOW), 1)
    dist = qi + WINDOW - si
    first_key = jnp.where(n > 0, 0, WINDOW)
    valid = (dist >= 0) & (dist < WINDOW) & (si >= first_key)
    dist_m = jnp.where(valid, dist.astype(F32), jnp.inf)
    outs = []
    for h in range(nq):
        j = h // group
        qh = q[:, h * HEAD:(h + 1) * HEAD]
        s = _dot_nt(qh, k2[:, j * HEAD:(j + 1) * HEAD]) - _alibi_slope(h, nq) * dist_m
        sink = sink_ref[h]
        mx = jnp.maximum(jnp.max(s, axis=-1, keepdims=True), sink)
        p = jnp.exp(s - mx)
        den = jnp.sum(p, axis=-1, keepdims=True) + jnp.exp(sink - mx)
        outs.append(_dot(p, v2[:, j * HEAD:(j + 1) * HEAD]) / den)
    o_ref[...] = jnp.concatenate(outs, axis=1).astype(o_ref.dtype)


def _attn_prompt(z, sinks, nb, t, dq, q_blk, k_blk, v_blk):
    nblk = t // WINDOW
    dkv = N_KV * HEAD
    cur = lambda col: (lambda b, n: (b * nblk + n, col))
    prev = lambda col: (lambda b, n: (b * nblk + jnp.maximum(n - 1, 0), col))
    vmem = 4 * WINDOW * (dq * 2 + 4 * dkv) * 4 + (24 << 20)
    return pl.pallas_call(
        _attn_prompt_kernel,
        grid=(nb, nblk),
        in_specs=[pl.BlockSpec(memory_space=pltpu.SMEM),
                  pl.BlockSpec((WINDOW, dq), cur(q_blk)),
                  pl.BlockSpec((WINDOW, dkv), prev(k_blk)),
                  pl.BlockSpec((WINDOW, dkv), cur(k_blk)),
                  pl.BlockSpec((WINDOW, dkv), prev(v_blk)),
                  pl.BlockSpec((WINDOW, dkv), cur(v_blk))],
        out_specs=pl.BlockSpec((WINDOW, dq), lambda b, n: (b * nblk + n, 0)),
        out_shape=jax.ShapeDtypeStruct((nb * t, dq), BF16),
        compiler_params=_cparams(("arbitrary", "arbitrary"), vmem),
        name="attn_prompt",
    )(sinks, z, z, z, z, z)


def _attn_sample_kernel(q_ref, kn_ref, vn_ref, ck_ref, cv_ref, sink_ref, slope_ref,
                        o_ref, ko_ref, vo_ref):
    bb, nq = q_ref.shape[0], q_ref.shape[1]
    group = nq // N_KV
    dkv = N_KV * HEAD
    hrow = lax.broadcasted_iota(jnp.int32, (nq, dkv), 0) // group
    hlane = lax.broadcasted_iota(jnp.int32, (nq, dkv), 1) >> 6
    own = hrow == hlane
    si = lax.broadcasted_iota(jnp.int32, (nq, WINDOW), 1)
    distf = (WINDOW - si).astype(F32)
    valid = si >= 1
    sink = sink_ref[...]
    slope = slope_ref[...]

    def body(b, carry):
        q = q_ref[b]
        q_bd = jnp.where(own, jnp.concatenate([q] * N_KV, axis=1), 0.0)
        ck = ck_ref[b]
        cv = cv_ref[b]
        kn = kn_ref[b]
        vn = vn_ref[b]
        s_c = _dot_nt(q_bd, ck) * SCALE - slope * distf
        s_c = jnp.where(valid, s_c, -jnp.inf)
        s_n = jnp.sum(q_bd * kn, axis=-1, keepdims=True) * SCALE
        mx = jnp.maximum(jnp.maximum(jnp.max(s_c, axis=-1, keepdims=True), s_n), sink)
        p_c = jnp.exp(s_c - mx)
        p_n = jnp.exp(s_n - mx)
        den = jnp.sum(p_c, axis=-1, keepdims=True) + p_n + jnp.exp(sink - mx)
        o_all = (_dot(p_c, cv) + p_n * vn) / den
        o_ref[b] = jnp.concatenate(
            [o_all[j * group:(j + 1) * group, j * HEAD:(j + 1) * HEAD] for j in range(N_KV)], axis=0)
        ko_ref[b] = jnp.concatenate([ck[1:], kn], axis=0)
        vo_ref[b] = jnp.concatenate([cv[1:], vn], axis=0)
        return carry

    lax.fori_loop(0, bb, body, 0)


def _attn_sample(q, kn, vn, ck, cv, sinks, slopes, bb):
    nb, nq = q.shape[0], q.shape[1]
    dkv = N_KV * HEAD
    cache = pl.BlockSpec((bb, WINDOW, dkv), lambda i: (i, 0, 0))
    new = pl.BlockSpec((bb, 1, dkv), lambda i: (i, 0, 0))
    qs = pl.BlockSpec((bb, nq, HEAD), lambda i: (i, 0, 0))
    vmem = 8 * bb * WINDOW * dkv * 4 + (16 << 20)
    return pl.pallas_call(
        _attn_sample_kernel,
        grid=(nb // bb,),
        in_specs=[qs, new, new, cache, cache, _const_spec((nq, 1)), _const_spec((nq, 1))],
        out_specs=[qs, cache, cache],
        out_shape=[jax.ShapeDtypeStruct((nb, nq, HEAD), F32),
                   jax.ShapeDtypeStruct(ck.shape, F32),
                   jax.ShapeDtypeStruct(cv.shape, F32)],
        compiler_params=_cparams(("arbitrary",), vmem),
        name="attn_sample",
    )(q, kn, vn, ck, cv, sinks, slopes)


def _out_proj_kernel(a_ref, b_ref, x_ref, w_ref, o_ref):
    d = a_ref.shape[1]
    o_ref[...] = (x_ref[...]
                  + jnp.dot(a_ref[...].astype(BF16), w_ref[0:d, :], preferred_element_type=F32)
                  + jnp.dot(b_ref[...].astype(BF16), w_ref[d:, :], preferred_element_type=F32))


def _out_proj(a, b, x, w, tm):
    m, d = a.shape
    dm = x.shape[1]
    vmem = 4 * tm * d * 4 + 4 * tm * dm * 4 + 2 * 2 * d * dm * 2 + (8 << 20)
    return pl.pallas_call(
        _out_proj_kernel,
        grid=(m // tm,),
        in_specs=[pl.BlockSpec((tm, d), lambda i: (i, 0)),
                  pl.BlockSpec((tm, d), lambda i: (i, 0)),
                  pl.BlockSpec((tm, dm), lambda i: (i, 0)),
                  _const_spec((2 * d, dm))],
        out_specs=pl.BlockSpec((tm, dm), lambda i: (i, 0)),
        out_shape=jax.ShapeDtypeStruct((m, dm), F32),
        compiler_params=_cparams(("arbitrary",), vmem),
        name="out_proj",
    )(a, b, x, w)


def _ffn_kernel(x_ref, g_ref, wg_ref, wu_ref, wd_ref, o_ref, h_ref, acc_ref):
    j = pl.program_id(1)

    @pl.when(j == 0)
    def _():
        h_ref[...] = _rms(x_ref[...], g_ref[...]).astype(BF16)
        acc_ref[...] = jnp.zeros_like(acc_ref)

    h = h_ref[...]
    gate = jnp.dot(h, wg_ref[...], preferred_element_type=F32)
    up = jnp.dot(h, wu_ref[...], preferred_element_type=F32)
    act = gate * _sigmoid(gate) * up
    acc_ref[...] += jnp.dot(act.astype(BF16), wd_ref[...], preferred_element_type=F32)

    @pl.when(j == pl.num_programs(1) - 1)
    def _():
        o_ref[...] = x_ref[...] + acc_ref[...]


def _ffn(x, gain, wg, wu, wd, tm, tf):
    m, dm = x.shape
    f = wg.shape[1]
    vmem = 4 * tm * dm * 4 + tm * dm * 2 + tm * dm * 4 + 6 * dm * tf * 2 + 4 * tm * tf * 4 + (6 << 20)
    return pl.pallas_call(
        _ffn_kernel,
        grid=(m // tm, f // tf),
        in_specs=[pl.BlockSpec((tm, dm), lambda i, j: (i, 0)),
                  _const_spec((1, dm)),
                  pl.BlockSpec((dm, tf), lambda i, j: (0, j)),
                  pl.BlockSpec((dm, tf), lambda i, j: (0, j)),
                  pl.BlockSpec((tf, dm), lambda i, j: (j, 0))],
        out_specs=pl.BlockSpec((tm, dm), lambda i, j: (i, 0)),
        out_shape=jax.ShapeDtypeStruct((m, dm), F32),
        scratch_shapes=[pltpu.VMEM((tm, dm), BF16), pltpu.VMEM((tm, dm), F32)],
        compiler_params=_cparams(("arbitrary", "arbitrary"), vmem),
        name="ffn",
    )(x, gain, wg, wu, wd)


def _ple_kernel(x_ref, p_ref, gn_ref, wgate_ref, wproj_ref, fn_ref, o_ref):
    x = x_ref[...]
    gate = _sigmoid(jnp.dot(_rms(x, gn_ref[...]).astype(BF16), wgate_ref[...],
                            preferred_element_type=F32))
    emb = jnp.dot(p_ref[...].astype(BF16), wproj_ref[...], preferred_element_type=F32)
    o_ref[...] = _rms(x + gate * emb, fn_ref[...])


def _ple_final(x, p, gn, wgate, wproj, fn, tm):
    m, dm = x.shape
    dp = p.shape[1]
    vmem = 4 * tm * dm * 4 + 2 * tm * dp * 4 + 2 * dm * dm * 2 + 2 * dp * dm * 2 + 4 * tm * dm * 4 + (6 << 20)
    return pl.pallas_call(
        _ple_kernel,
        grid=(m // tm,),
        in_specs=[pl.BlockSpec((tm, dm), lambda i: (i, 0)),
                  pl.BlockSpec((tm, dp), lambda i: (i, 0)),
                  _const_spec((1, dm)), _const_spec((dm, dm)), _const_spec((dp, dm)),
                  _const_spec((1, dm))],
        out_specs=pl.BlockSpec((tm, dm), lambda i: (i, 0)),
        out_shape=jax.ShapeDtypeStruct((m, dm), F32),
        compiler_params=_cparams(("arbitrary",), vmem),
        name="ple_final",
    )(x, p, gn, wgate, wproj, fn)


def _pad_cols(a, n):
    return jnp.pad(a, ((0, 0), (0, n - a.shape[1])))


def _layer(x, p_emb, lw_, final_norm, *, nb, t, state, shift_prev, caches):
    m, dm = x.shape
    d = lw_["d"]
    dq = lw_["dq"]
    dkv = N_KV * HEAD
    sample = caches is not None
    o_q, o_k, o_v, o_l = 3 * d, 3 * d + dq, 3 * d + dq + dkv, 3 * d + dq + 2 * dkv
    tm = min(m, 1024)
    z = _norm_matmul(x, lw_["norm_mix"], lw_["w_in"], tm, 1280)
    lora_blk = o_l // 512
    nh = d // HEAD
    if not sample:
        feats = _rwkv_prep_prompt(z, lw_["prep"], nb, t, d, 256, lora_blk)
        out_r, wkv = _rwkv_scan(*feats, lw_["r_k"], lw_["ln_w"], lw_["ln_b"], nb, t, d)
        y_a = _attn_prompt(z, lw_["sinks"], nb, t, dq, o_q // dq, o_k // dkv, o_v // dkv)
        z3 = z.reshape(nb, t, -1)
        shift_rows = z3[:, -1].astype(F32)
        k_new = z3[:, -WINDOW:, o_k:o_k + dkv].astype(F32).reshape(nb, WINDOW, N_KV, HEAD)
        v_new = z3[:, -WINDOW:, o_v:o_v + dkv].astype(F32).reshape(nb, WINDOW, N_KV, HEAD)
    else:
        prev_r = shift_prev[:, :3 * d]
        prev_l = jnp.concatenate(
            [_pad_cols(shift_prev[:, 3 * d:3 * d + LORA_W], LORA_PAD),
             _pad_cols(shift_prev[:, 3 * d + LORA_W:3 * d + LORA_W + LORA_A], LORA_PAD),
             shift_prev[:, 3 * d + LORA_W + LORA_A:]], axis=1)
        feats = _rwkv_prep_sample(z, prev_r, prev_l, lw_["prep"], d, lora_blk)
        feats3 = [f.reshape(m, nh, HEAD) for f in feats]
        wkv, out_r3 = _rwkv_step(state, *feats3, lw_["r_k"].reshape(nh, HEAD),
                                 lw_["ln_w"].reshape(nh, HEAD), lw_["ln_b"].reshape(nh, HEAD), 8)
        out_r = out_r3.reshape(m, d)
        ck, cv = caches
        q3 = z[:, o_q:o_q + dq].astype(F32).reshape(m, dq // HEAD, HEAD)
        kn = z[:, o_k:o_k + dkv].astype(F32).reshape(m, 1, dkv)
        vn = z[:, o_v:o_v + dkv].astype(F32).reshape(m, 1, dkv)
        y3, k_new, v_new = _attn_sample(q3, kn, vn, ck.reshape(m, WINDOW, dkv),
                                        cv.reshape(m, WINDOW, dkv), lw_["sinks"].reshape(-1, 1),
                                        lw_["slopes"], 8)
        y_a = y3.reshape(m, dq)
        k_new = k_new.reshape(m, WINDOW, N_KV, HEAD)
        v_new = v_new.reshape(m, WINDOW, N_KV, HEAD)
        shift_rows = z.astype(F32)
    shift_new = jnp.concatenate(
        [shift_rows[:, :3 * d], shift_rows[:, o_l:o_l + LORA_W],
         shift_rows[:, o_l + LORA_PAD:o_l + LORA_PAD + LORA_A],
         shift_rows[:, o_l + 2 * LORA_PAD:o_l + 2 * LORA_PAD + LORA_G]], axis=1)
    tm2 = min(m, 512)
    x1 = _out_proj(out_r, y_a, x, lw_["w_out"], tm2)
    x2 = _ffn(x1, lw_["norm_ffn"], lw_["w_gate"], lw_["w_up"], lw_["w_down"], tm2, 512)
    y = _ple_final(x2, p_emb, lw_["norm_ple"], lw_["ple_gate"], lw_["ple_proj"], final_norm, tm2)
    return y, wkv, shift_new, k_new, v_new


def kernel(x_prompt, x_sample, state_wkv, state_shift, cache_k, cache_v, p_prompt, p_sample, norm_mix, w_in, mu_shift, rwkv_w0, rwkv_w2, rwkv_a0, rwkv_a2, rwkv_g2, rwkv_k_k, rwkv_k_a, rwkv_r_k, rwkv_ln_w, rwkv_ln_b, attn_sinks, w_out, norm_ffn, w_gate, w_up, w_down, norm_ple, ple_gate, ple_proj, final_norm):
    depth = w_in.shape[0]
    assert depth == 1, "single-layer step"
    bp, t, dm = x_prompt.shape
    bs = x_sample.shape[0]
    assert x_sample.shape[1] == 1
    d = rwkv_w0.shape[1]
    dq = attn_sinks.shape[1] * HEAD
    dkv = N_KV * HEAD
    i = 0
    row = lambda a: a.reshape(1, -1)
    wi = w_in[i]
    rp = 3 * d + LORA_W + LORA_A + LORA_G
    w_in_p = jnp.concatenate(
        [wi[:, :3 * d], wi[:, rp:],
         _pad_cols(wi[:, 3 * d:3 * d + LORA_W], LORA_PAD),
         _pad_cols(wi[:, 3 * d + LORA_W:3 * d + LORA_W + LORA_A], LORA_PAD),
         wi[:, 3 * d + LORA_W + LORA_A:rp]], axis=1).astype(BF16)
    mu = row(mu_shift[i])
    mu_r = mu[:, :3 * d]
    mu_l = jnp.concatenate(
        [_pad_cols(mu[:, 3 * d:3 * d + LORA_W], LORA_PAD),
         _pad_cols(mu[:, 3 * d + LORA_W:3 * d + LORA_W + LORA_A], LORA_PAD),
         mu[:, 3 * d + LORA_W + LORA_A:]], axis=1)
    pad_rows = lambda a: jnp.pad(a, ((0, LORA_PAD - a.shape[0]), (0, 0))).astype(BF16)
    nq = dq // HEAD
    lw_ = {
        "d": d, "dq": dq,
        "norm_mix": row(norm_mix[i]), "w_in": w_in_p,
        "prep": (mu_r, mu_l, row(rwkv_w0[i]), pad_rows(rwkv_w2[i]), row(rwkv_a0[i]),
                 pad_rows(rwkv_a2[i]), rwkv_g2[i].astype(BF16), row(rwkv_k_k[i]), row(rwkv_k_a[i])),
        "r_k": row(rwkv_r_k[i]), "ln_w": row(rwkv_ln_w[i]), "ln_b": row(rwkv_ln_b[i]),
        "sinks": attn_sinks[i],
        "slopes": (2.0 ** (-8.0 * jnp.arange(1, nq + 1, dtype=F32) / nq)).reshape(nq, 1),
        "w_out": w_out[i].astype(BF16), "norm_ffn": row(norm_ffn[i]),
        "w_gate": w_gate[i].astype(BF16), "w_up": w_up[i].astype(BF16),
        "w_down": w_down[i].astype(BF16), "norm_ple": row(norm_ple[i]),
        "ple_gate": ple_gate[i].astype(BF16), "ple_proj": ple_proj[i].astype(BF16),
    }
    fn = row(final_norm)
    yp, wkv_p, shift_p, k_p, v_p = _layer(
        x_prompt.reshape(bp * t, dm), p_prompt[i].reshape(bp * t, -1), lw_, fn,
        nb=bp, t=t, state=None, shift_prev=None, caches=None)
    ys, wkv_s, shift_s, k_s, v_s = _layer(
        x_sample.reshape(bs, dm), p_sample[i].reshape(bs, -1), lw_, fn,
        nb=bs, t=1, state=state_wkv[i], shift_prev=state_shift[i],
        caches=(cache_k[i], cache_v[i]))
    return (yp.reshape(bp, t, dm), ys.reshape(bs, 1, dm),
            wkv_p[None], shift_p[None], k_p[None], v_p[None],
            wkv_s[None], shift_s[None], k_s[None], v_s[None])
```
